```python
import math
import jax, jax.numpy as jnp
from jax import lax
import numpy as np

D_MODEL = 1024
BATCH = 16
SEQ = 2048
DEPTH = 2

HEAD_DIM = 64
A_HEADS = 6
A_PAIRS = ((128, 1), (512, 4), (2048, 16))
B_HEADS = 4
B_QK_DIM = 32
B_V_DIM = 2 * B_QK_DIM
C_HEADS = 6
C_Q_LORA = 256
C_KV_LORA = 128
C_NOPE = 64
C_ROPE = 32
C_V = 64
ROPE_THETA = 10000.0
MIX_WIDTH = (A_HEADS + B_HEADS + C_HEADS) * HEAD_DIM
A_COLS = 3 * A_HEADS * HEAD_DIM
B_COLS = B_HEADS * (2 * 2 * B_QK_DIM + B_V_DIM)
C_COLS = C_Q_LORA + C_KV_LORA + C_ROPE
IN_COLS = A_COLS + B_COLS + C_COLS
NUM_BUCKETS = 32
MAX_DISTANCE = 2048
BIAS_HEADS = A_HEADS + B_HEADS
FF_DIM = -(-8 * D_MODEL // (3 * 256)) * 256
Q_BLOCK = 128
DEEPNORM_ALPHA = (2 * DEPTH) ** 0.25
DEEPNORM_BETA = (8 * DEPTH) ** -0.25
LN_EPS = 1e-5
LATENT_EPS = 1e-6
SUBLN_EPS = 1e-5

kernel_name = "hybrid_dilated_diff_mla_deepnorm"


def layer_norm(x, g, b):
    xf = x.astype(jnp.float32)
    mu = jnp.mean(xf, axis=-1, keepdims=True)
    var = jnp.mean(jnp.square(xf - mu), axis=-1, keepdims=True)
    return ((xf - mu) * lax.rsqrt(var + LN_EPS) * g + b).astype(x.dtype)


def rms_norm(x, g, eps):
    xf = x.astype(jnp.float32)
    return (xf * lax.rsqrt(jnp.mean(xf * xf, axis=-1, keepdims=True) + eps) * g).astype(x.dtype)


def t5_bucket(dist):
    max_exact = NUM_BUCKETS // 2
    d = jnp.maximum(dist, 0)
    df = jnp.maximum(d, 1).astype(jnp.float32)
    large = max_exact + (jnp.log(df / max_exact) / math.log(MAX_DISTANCE / max_exact)
                         * (NUM_BUCKETS - max_exact)).astype(jnp.int32)
    large = jnp.minimum(large, NUM_BUCKETS - 1)
    return jnp.where(d < max_exact, d, large)


def apply_rope(x):
    S, half = x.shape[1], x.shape[-1] // 2
    inv = ROPE_THETA ** (-jnp.arange(half, dtype=jnp.float32) / half)
    ang = jnp.arange(S, dtype=jnp.float32)[:, None] * inv[None, :]
    cos, sin = jnp.cos(ang)[None, :, None, :], jnp.sin(ang)[None, :, None, :]
    xf = x.astype(jnp.float32)
    x1, x2 = xf[..., :half], xf[..., half:]
    return jnp.concatenate([x1 * cos - x2 * sin, x2 * cos + x1 * sin], axis=-1).astype(x.dtype)


def dilated_pair(q, k, v, bias_tab, window, dil):
    Bn, S, H, Dh = q.shape
    blk = window // dil
    L = S // dil
    nb = -(-L // blk)
    pad = nb * blk - L

    def to_blocks(a):
        a = a.reshape(Bn, L, dil, H, Dh)
        a = jnp.pad(a, ((0, 0), (0, pad), (0, 0), (0, 0), (0, 0)))
        return a.reshape(Bn, nb, blk, dil, H, Dh)

    def with_prev(a):
        prev = jnp.pad(a, ((0, 0), (1, 0), (0, 0), (0, 0), (0, 0), (0, 0)))[:, :nb]
        return jnp.concatenate([prev, a], axis=2)

    qb = to_blocks(q)
    kb = with_prev(to_blocks(k))
    vb = with_prev(to_blocks(v))
    rel = jnp.arange(blk)[:, None] + blk - jnp.arange(2 * blk)[None, :]
    band = (rel >= 0) & (rel <= blk)
    not_first = (jnp.arange(nb)[:, None, None] > 0) | (jnp.arange(2 * blk)[None, None, :] >= blk)
    mask = band[None] & not_first
    bias = bias_tab[t5_bucket(rel * dil)].astype(jnp.float32).transpose(2, 0, 1)
    s = jnp.einsum('bnqrhd,bnkrhd->bnrhqk', qb, kb).astype(jnp.float32) * (Dh ** -0.5) + bias
    s = jnp.where(mask[None, :, None, None], s, -jnp.inf)
    lse = jax.nn.logsumexp(s, axis=-1)
    p = jnp.exp(s - lse[..., None]).astype(v.dtype)
    o = jnp.einsum('bnrhqk,bnkrhd->bnqrhd', p, vb)
    o = o.reshape(Bn, nb * blk, dil, H, Dh)[:, :L].reshape(Bn, S, H, Dh)
    lse = lse.transpose(0, 1, 4, 2, 3).reshape(Bn, nb * blk, dil, H)[:, :L].reshape(Bn, S, H)
    return o, lse


def dilated_mixture(q, k, v, bias_tab):
    outs, lses = [], []
    for window, dil in A_PAIRS:
        o, lse = dilated_pair(q, k, v, bias_tab, window, dil)
        outs.append(o)
        lses.append(lse)
    w = jax.nn.softmax(jnp.stack(lses, axis=0), axis=0)
    o = jnp.sum(w[..., None] * jnp.stack(outs, axis=0).astype(jnp.float32), axis=0)
    return o.astype(q.dtype)


def sweep_query_blocks(block_fn, n_pos):
    o = lax.map(block_fn, jnp.arange(n_pos // Q_BLOCK))
    o = jnp.moveaxis(o, 0, 1)
    return o.reshape((o.shape[0], n_pos) + o.shape[3:])


def diff_attention(q1, q2, k1, k2, v, lam, bias_tab):
    S = q1.shape[1]
    scale = q1.shape[-1] ** -0.5
    kpos = jnp.arange(S)

    def block(i):
        start = i * Q_BLOCK
        sl = lambda a: lax.dynamic_slice_in_dim(a, start, Q_BLOCK, axis=1)
        rel = (start + jnp.arange(Q_BLOCK))[:, None] - kpos[None, :]
        causal = rel >= 0
        bias = bias_tab[t5_bucket(rel)].astype(jnp.float32).transpose(2, 0, 1)

        def probs(q, k):
            s = jnp.einsum('bqhd,bkhd->bhqk', sl(q), k).astype(jnp.float32) * scale + bias
            return jax.nn.softmax(jnp.where(causal, s, -jnp.inf), axis=-1)

        a = probs(q1, k1) - lam * probs(q2, k2)
        return jnp.einsum('bhqk,bkhd->bqhd', a.astype(v.dtype), v)

    return sweep_query_blocks(block, S)


def causal_attention(q, k, v):
    S = q.shape[1]
    scale = q.shape[-1] ** -0.5
    kpos = jnp.arange(S)

    def block(i):
        start = i * Q_BLOCK
        qs = lax.dynamic_slice_in_dim(q, start, Q_BLOCK, axis=1)
        causal = (start + jnp.arange(Q_BLOCK))[:, None] >= kpos[None, :]
        s = jnp.einsum('bqhd,bkhd->bhqk', qs, k).astype(jnp.float32) * scale
        p = jax.nn.softmax(jnp.where(causal, s, -jnp.inf), axis=-1)
        return jnp.einsum('bhqk,bkhd->bqhd', p.astype(v.dtype), v)

    return sweep_query_blocks(block, S)


def hybrid_layer(x, layer_idx, rel_bias, w_in, q_norm_g, kv_norm_g, w_uq, w_ukv,
                 diff_lambda, subln_g, w_o, ln1_g, ln1_b, ln2_g, ln2_b, w_gate, w_up, w_down):
    Bn, S, _ = x.shape
    proj = x @ w_in
    a_in, b_in, c_in = jnp.split(proj, [A_COLS, A_COLS + B_COLS], axis=-1)

    a = a_in.reshape(Bn, S, 3, A_HEADS, HEAD_DIM)
    o_a = dilated_mixture(a[:, :, 0], a[:, :, 1], a[:, :, 2], rel_bias[:, :A_HEADS])

    qk_cols = B_HEADS * 2 * B_QK_DIM
    bq = b_in[..., :qk_cols].reshape(Bn, S, B_HEADS, 2, B_QK_DIM)
    bk = b_in[..., qk_cols:2 * qk_cols].reshape(Bn, S, B_HEADS, 2, B_QK_DIM)
    bv = b_in[..., 2 * qk_cols:].reshape(Bn, S, B_HEADS, B_V_DIM)
    lam_init = 0.8 - 0.6 * math.exp(-0.3 * layer_idx)
    lf = diff_lambda.astype(jnp.float32)
    lam = jnp.exp(jnp.sum(lf[0] * lf[1])) - jnp.exp(jnp.sum(lf[2] * lf[3])) + lam_init
    o_b = diff_attention(bq[..., 0, :], bq[..., 1, :], bk[..., 0, :], bk[..., 1, :], bv, lam,
                         rel_bias[:, A_HEADS:])
    o_b = (rms_norm(o_b, subln_g, SUBLN_EPS) * (1.0 - lam_init)).astype(x.dtype)

    c_q, c_kv, k_r = jnp.split(c_in, [C_Q_LORA, C_Q_LORA + C_KV_LORA], axis=-1)
    q = (rms_norm(c_q, q_norm_g, LATENT_EPS) @ w_uq).reshape(Bn, S, C_HEADS, C_NOPE + C_ROPE)
    kv = (rms_norm(c_kv, kv_norm_g, LATENT_EPS) @ w_ukv).reshape(Bn, S, C_HEADS, C_NOPE + C_V)
    q_c = jnp.concatenate([q[..., :C_NOPE], apply_rope(q[..., C_NOPE:])], axis=-1)
    k_rope = apply_rope(k_r[:, :, None, :])
    k_c = jnp.concatenate([kv[..., :C_NOPE],
                           jnp.broadcast_to(k_rope, (Bn, S, C_HEADS, C_ROPE))], axis=-1)
    o_c = causal_attention(q_c, k_c, kv[..., C_NOPE:])

    heads = jnp.concatenate([o_a.reshape(Bn, S, -1), o_b.reshape(Bn, S, -1),
                             o_c.reshape(Bn, S, -1)], axis=-1)
    h = layer_norm(DEEPNORM_ALPHA * x + heads @ w_o, ln1_g, ln1_b)

    ffn = (jax.nn.silu(h @ w_gate) * (h @ w_up)) @ w_down
    return layer_norm(DEEPNORM_ALPHA * h + ffn, ln2_g, ln2_b)


def setup_inputs(seed: int = 0) -> dict:
    key = jax.random.key(seed)
    ks = jax.random.split(key, 20)
    f32 = jnp.float32
    nrm = lambda k, shape, scale: jax.random.normal(k, shape, f32) * scale
    return {
        "x": nrm(ks[0], (BATCH, SEQ, D_MODEL), 1.0),
        "rel_bias": nrm(ks[1], (NUM_BUCKETS, BIAS_HEADS), 0.5),
        "w_in": nrm(ks[2], (DEPTH, D_MODEL, IN_COLS), D_MODEL ** -0.5),
        "q_norm_g": 1.0 + nrm(ks[3], (DEPTH, C_Q_LORA), 0.05),
        "kv_norm_g": 1.0 + nrm(ks[4], (DEPTH, C_KV_LORA), 0.05),
        "w_uq": nrm(ks[5], (DEPTH, C_Q_LORA, C_HEADS * (C_NOPE + C_ROPE)), C_Q_LORA ** -0.5),
        "w_ukv": nrm(ks[6], (DEPTH, C_KV_LORA, C_HEADS * (C_NOPE + C_V)), C_KV_LORA ** -0.5),
        "diff_lambda": nrm(ks[7], (DEPTH, 4, B_QK_DIM), 0.1),
        "subln_g": 1.0 + nrm(ks[8], (DEPTH, B_V_DIM), 0.05),
        "w_o": nrm(ks[9], (DEPTH, MIX_WIDTH, D_MODEL), MIX_WIDTH ** -0.5 * DEEPNORM_BETA),
        "ln1_g": 1.0 + nrm(ks[10], (DEPTH, D_MODEL), 0.05),
        "ln1_b": nrm(ks[11], (DEPTH, D_MODEL), 0.02),
        "ln2_g": 1.0 + nrm(ks[12], (DEPTH, D_MODEL), 0.05),
        "ln2_b": nrm(ks[13], (DEPTH, D_MODEL), 0.02),
        "w_gate": nrm(ks[14], (DEPTH, D_MODEL, FF_DIM), D_MODEL ** -0.5),
        "w_up": nrm(ks[15], (DEPTH, D_MODEL, FF_DIM), D_MODEL ** -0.5),
        "w_down": nrm(ks[16], (DEPTH, FF_DIM, D_MODEL), FF_DIM ** -0.5 * DEEPNORM_BETA),
    }


def reference(x, rel_bias, w_in, q_norm_g, kv_norm_g, w_uq, w_ukv, diff_lambda, subln_g, w_o,
              ln1_g, ln1_b, ln2_g, ln2_b, w_gate, w_up, w_down):
    for l in range(DEPTH):
        x = hybrid_layer(x, l, rel_bias, w_in[l], q_norm_g[l], kv_norm_g[l], w_uq[l], w_ukv[l],
                         diff_lambda[l], subln_g[l], w_o[l], ln1_g[l], ln1_b[l], ln2_g[l], ln2_b[l],
                         w_gate[l], w_up[l], w_down[l])
    return x
```

```python
import functools
import math

import numpy as np
import jax
import jax.numpy as jnp
from jax import lax
from jax.experimental import pallas as pl
from jax.experimental.pallas import tpu as pltpu

F32 = jnp.float32
BF16 = jnp.bfloat16

D_MODEL = 1024
HEAD_DIM = 64
A_HEADS = 6
A_PAIRS = ((128, 1), (512, 4), (2048, 16))
A_BLK = 128
B_HEADS = 4
B_QK_DIM = 32
C_HEADS = 6
C_Q_LORA = 256
C_KV_LORA = 128
C_NOPE = 64
C_ROPE = 32
ROPE_THETA = 10000.0
A_COLS = 3 * A_HEADS * HEAD_DIM
B_COLS = B_HEADS * 3 * HEAD_DIM
NUM_BUCKETS = 32
MAX_DISTANCE = 2048
FF_DIM = 2816
LN_EPS = 1e-5
LATENT_EPS = 1e-6
SUBLN_EPS = 1e-5

LANES = 128
MASK_VALUE = -1e30
VMEM_LIMIT = 56 * 1024 * 1024

PROJ_TM = 512
FFN_TM = 512
FFN_CHUNK = 1408
BQ = 256
BK = 256


def _bucket_lower_bounds():
    max_exact = NUM_BUCKETS // 2
    d = np.arange(0, MAX_DISTANCE + 1)
    val = np.log(np.maximum(d, 1) / max_exact) / math.log(MAX_DISTANCE / max_exact) * (NUM_BUCKETS - max_exact)
    large = np.minimum(max_exact + np.floor(np.maximum(val, 0.0)).astype(np.int64), NUM_BUCKETS - 1)
    bucket = np.where(d < max_exact, d, large)
    return [int(np.argmax(bucket >= b)) for b in range(NUM_BUCKETS)]


_BUCKET_LO = _bucket_lower_bounds()


def _nt_dot(a, b):
    return lax.dot_general(a, b, (((1,), (1,)), ((), ())), preferred_element_type=F32)


def _bias_tile_kernel(rb_ref, out_ref, *, head_offset, dilated):
    h = pl.program_id(0) + head_offset
    g = pl.program_id(1)
    rows, cols = out_ref.shape[2], out_ref.shape[3]
    i = lax.broadcasted_iota(jnp.int32, (rows, cols), 0)
    j = lax.broadcasted_iota(jnp.int32, (rows, cols), 1)
    if dilated:
        rel = i + A_BLK - j
        valid = (rel >= 0) & (rel <= A_BLK)
        dil = jnp.where(g == 0, A_PAIRS[0][1], jnp.where(g == 1, A_PAIRS[1][1], A_PAIRS[2][1]))
        dist = rel * dil
    else:
        dist = g * rows + i - j
        valid = dist >= 0
    val = jnp.full((rows, cols), rb_ref[0, h], F32)
    for b in range(1, NUM_BUCKETS):
        val = jnp.where(dist >= _BUCKET_LO[b], rb_ref[b, h], val)
    out_ref[0, 0] = jnp.where(valid, val, MASK_VALUE)


def _bias_tiles(rel_bias, n_heads, head_offset, n_groups, rows, cols, dilated):
    return pl.pallas_call(
        functools.partial(_bias_tile_kernel, head_offset=head_offset, dilated=dilated),
        grid=(n_heads, n_groups),
        in_specs=[pl.BlockSpec(memory_space=pltpu.SMEM)],
        out_specs=pl.BlockSpec((1, 1, rows, cols), lambda h, g: (h, g, 0, 0)),
        out_shape=jax.ShapeDtypeStruct((n_heads, n_groups, rows, cols), F32),
        name="bias_tiles",
    )(rel_bias)


def _proj_kernel(x_ref, w1_ref, qg_ref, kvg_ref, wq_ref, wkv_ref, cos_ref, sin_ref,
                 pa_ref, pb_ref, qc_ref, kc_ref, vc_ref):
    xb = x_ref[...].astype(BF16)
    pa_ref[...] = jnp.dot(xb, w1_ref[:, :A_COLS], preferred_element_type=F32)
    pb_ref[...] = jnp.dot(xb, w1_ref[:, A_COLS:A_COLS + B_COLS], preferred_element_type=F32).astype(BF16)
    c = jnp.dot(xb, w1_ref[:, A_COLS + B_COLS:], preferred_element_type=F32)
    c_q = c[:, :C_Q_LORA]
    c_kv = c[:, C_Q_LORA:C_Q_LORA + C_KV_LORA]
    k_rot = c[:, C_Q_LORA + C_KV_LORA:C_Q_LORA + C_KV_LORA + LANES]
    k_rot_half = c[:, C_Q_LORA + C_KV_LORA + LANES:]
    cqn = (c_q * lax.rsqrt(jnp.mean(c_q * c_q, axis=-1, keepdims=True) + LATENT_EPS) * qg_ref[...]).astype(BF16)
    ckvn = (c_kv * lax.rsqrt(jnp.mean(c_kv * c_kv, axis=-1, keepdims=True) + LATENT_EPS) * kvg_ref[...]).astype(BF16)
    q2 = jnp.dot(cqn, wq_ref[...], preferred_element_type=F32)
    kv = jnp.dot(ckvn, wkv_ref[...], preferred_element_type=F32)
    cos = cos_ref[...]
    sin = sin_ref[...]
    k_rope = k_rot * cos + k_rot_half * sin
    qw = C_HEADS * LANES
    for h in range(C_HEADS):
        sl = slice(h * LANES, (h + 1) * LANES)
        sl2 = slice(qw + h * LANES, qw + (h + 1) * LANES)
        qc_ref[:, sl] = (q2[:, sl] * cos + q2[:, sl2] * sin).astype(BF16)
        kc_ref[:, sl] = (kv[:, sl] + k_rope).astype(BF16)
    vc_ref[...] = kv[:, qw:].astype(BF16)


def _projection(x2d, w1, qg, kvg, wq, wkv, cos_t, sin_t, seq):
    n_tok = x2d.shape[0]
    tm = PROJ_TM
    pos_blocks = seq // tm
    full = lambda shape: pl.BlockSpec(shape, lambda i: (0, 0))
    tok = lambda width: pl.BlockSpec((tm, width), lambda i: (i, 0))
    pos = pl.BlockSpec((tm, LANES), lambda i: (i % pos_blocks, 0))
    return pl.pallas_call(
        _proj_kernel,
        grid=(n_tok // tm,),
        in_specs=[tok(D_MODEL), full(w1.shape), full(qg.shape), full(kvg.shape), full(wq.shape),
                  full(wkv.shape), pos, pos],
        out_specs=[tok(A_COLS), tok(B_COLS), tok(C_HEADS * LANES), tok(C_HEADS * LANES),
                   tok(C_HEADS * HEAD_DIM)],
        out_shape=[jax.ShapeDtypeStruct((n_tok, A_COLS), F32),
                   jax.ShapeDtypeStruct((n_tok, B_COLS), BF16),
                   jax.ShapeDtypeStruct((n_tok, C_HEADS * LANES), BF16),
                   jax.ShapeDtypeStruct((n_tok, C_HEADS * LANES), BF16),
                   jax.ShapeDtypeStruct((n_tok, C_HEADS * HEAD_DIM), BF16)],
        compiler_params=pltpu.CompilerParams(dimension_semantics=("arbitrary",),
                                             vmem_limit_bytes=VMEM_LIMIT),
        name="projection",
    )(x2d, w1, qg, kvg, wq, wkv, cos_t, sin_t)


def _attn_a_kernel(q_ref, k_ref, v_ref, bias_ref, o_ref, os_ref, ls_ref, ms_ref):
    seq = q_ref.shape[1]
    lane = lax.broadcasted_iota(jnp.int32, (A_BLK, LANES), 1)
    lo_half = lane < HEAD_DIM
    ones = jnp.ones((2 * A_BLK, LANES), BF16)
    scale = HEAD_DIM ** -0.5

    def rows(start, size, dil):
        if dil == 1:
            return pl.ds(start, size)
        return pl.ds(start, size, stride=dil)

    def block(p, dil, q_start, first):
        qf = q_ref[0, rows(q_start, A_BLK, dil), :]
        if first:
            k_start, nk = q_start, A_BLK
        else:
            k_start, nk = q_start - A_BLK * dil, 2 * A_BLK
        kb = k_ref[0, rows(k_start, nk, dil), :].astype(BF16)
        vb = v_ref[0, rows(k_start, nk, dil), :].astype(BF16)
        v_aug = jnp.concatenate([vb, ones[:nk]], axis=1)
        acc, den, mx = [], [], []
        for j in range(2):
            qj = jnp.where(lo_half if j == 0 else jnp.logical_not(lo_half), qf, 0.0).astype(BF16)
            bias = bias_ref[j, p]
            if first:
                bias = bias[:, A_BLK:]
            s = _nt_dot(qj, kb) * scale + bias
            m = jnp.max(s, axis=-1, keepdims=True)
            pr = jnp.exp(s - m).astype(BF16)
            r = jnp.dot(pr, v_aug, preferred_element_type=F32)
            acc.append(r[:, :LANES])
            den.append(r[:, LANES:])
            mx.append(jnp.broadcast_to(m, (A_BLK, LANES)))
        dst = rows(q_start, A_BLK, dil)
        os_ref[p, dst, :] = jnp.where(lo_half, acc[0], acc[1])
        ls_ref[p, dst, :] = jnp.where(lo_half, den[0], den[1])
        ms_ref[p, dst, :] = jnp.where(lo_half, mx[0], mx[1])

    for p, (window, dil) in enumerate(A_PAIRS):
        n_blocks = seq // dil // A_BLK

        def residue_class(r, carry, p=p, dil=dil, n_blocks=n_blocks):
            block(p, dil, r, True)
            if n_blocks > 1:
                def later(n, c):
                    start = n * (A_BLK * dil) + r
                    if dil == 1:
                        start = pl.multiple_of(start, A_BLK)
                    block(p, dil, start, False)
                    return c
                lax.fori_loop(1, n_blocks, later, 0)
            return carry

        if dil == 1:
            residue_class(0, 0)
        else:
            lax.fori_loop(0, dil, residue_class, 0)

    chunk = 256

    def combine(c, carry):
        rs = pl.ds(pl.multiple_of(c * chunk, chunk), chunk)
        m0, m1, m2 = ms_ref[0, rs, :], ms_ref[1, rs, :], ms_ref[2, rs, :]
        mx = jnp.maximum(jnp.maximum(m0, m1), m2)
        e0, e1, e2 = jnp.exp(m0 - mx), jnp.exp(m1 - mx), jnp.exp(m2 - mx)
        num = e0 * os_ref[0, rs, :] + e1 * os_ref[1, rs, :] + e2 * os_ref[2, rs, :]
        den = e0 * ls_ref[0, rs, :] + e1 * ls_ref[1, rs, :] + e2 * ls_ref[2, rs, :]
        o_ref[0, rs, :] = (num / den).astype(o_ref.dtype)
        return carry

    lax.fori_loop(0, seq // chunk, combine, 0)


def _attention_a(proj_a, bias_a):
    batch, seq, _ = proj_a.shape
    pairs = A_HEADS // 2
    n_pairs = len(A_PAIRS)
    col = lambda off: pl.BlockSpec((1, seq, LANES), lambda hp, b: (b, 0, off + hp))
    return pl.pallas_call(
        _attn_a_kernel,
        grid=(pairs, batch),
        in_specs=[col(0), col(pairs), col(2 * pairs),
                  pl.BlockSpec((2, n_pairs, A_BLK, 2 * A_BLK), lambda hp, b: (hp, 0, 0, 0))],
        out_specs=pl.BlockSpec((1, seq, LANES), lambda hp, b: (b, 0, hp)),
        out_shape=jax.ShapeDtypeStruct((batch, seq, A_HEADS * HEAD_DIM), BF16),
        scratch_shapes=[pltpu.VMEM((n_pairs, seq, LANES), F32)] * 3,
        compiler_params=pltpu.CompilerParams(dimension_semantics=("arbitrary", "arbitrary"),
                                             vmem_limit_bytes=VMEM_LIMIT),
        name="attention_a",
    )(proj_a, proj_a, proj_a, bias_a)


def _softmax_step(s, v_aug, m, acc):
    m_new = jnp.maximum(m, jnp.max(s, axis=-1, keepdims=True))
    alpha = jnp.exp(m - m_new)
    pr = jnp.exp(s - m_new).astype(BF16)
    return m_new, acc * alpha + jnp.dot(pr, v_aug, preferred_element_type=F32)


def _attn_b_kernel(q_ref, k_ref, v_ref, bias_ref, lam_ref, g_ref, o_ref, *, lam_init):
    qi = pl.program_id(2)
    q = q_ref[0]
    lane = lax.broadcasted_iota(jnp.int32, (BQ, LANES), 1)
    ones = jnp.ones((BK, LANES), BF16)
    scale = B_QK_DIM ** -0.5
    lf = lam_ref[...]
    dot_rows = lambda a, b: jnp.sum(lf[a:a + 1] * lf[b:b + 1], axis=-1, keepdims=True)
    lam = jnp.exp(dot_rows(0, 1)) - jnp.exp(dot_rows(2, 3)) + lam_init

    outs = []
    for j in range(2):
        base = j * HEAD_DIM
        q1 = jnp.where((lane >= base) & (lane < base + B_QK_DIM), q, jnp.zeros_like(q))
        q2 = jnp.where((lane >= base + B_QK_DIM) & (lane < base + HEAD_DIM), q, jnp.zeros_like(q))

        def step(ki, carry, j=j, q1=q1, q2=q2):
            m1, a1, m2, a2 = carry
            off = pl.multiple_of(ki * BK, BK)
            k = k_ref[0, pl.ds(off, BK), :]
            v_aug = jnp.concatenate([v_ref[0, pl.ds(off, BK), :], ones], axis=1)
            bias = bias_ref[j, qi - ki]
            m1, a1 = _softmax_step(_nt_dot(q1, k) * scale + bias, v_aug, m1, a1)
            m2, a2 = _softmax_step(_nt_dot(q2, k) * scale + bias, v_aug, m2, a2)
            return m1, a1, m2, a2

        m0 = jnp.full((BQ, 1), MASK_VALUE, F32)
        a0 = jnp.zeros((BQ, 2 * LANES), F32)
        _, a1, _, a2 = lax.fori_loop(0, qi + 1, step, (m0, a0, m0, a0))
        o = a1[:, :LANES] / a1[:, LANES:] - lam * (a2[:, :LANES] / a2[:, LANES:])
        in_head = (lane >= base) & (lane < base + HEAD_DIM)
        ms = jnp.sum(jnp.where(in_head, o * o, 0.0), axis=-1, keepdims=True) * (1.0 / HEAD_DIM)
        outs.append(o * lax.rsqrt(ms + SUBLN_EPS) * g_ref[...] * (1.0 - lam_init))
    o_ref[0] = jnp.where(lane < HEAD_DIM, outs[0], outs[1]).astype(o_ref.dtype)


def _attention_b(proj_b, bias_b, diff_lambda, subln_g2, lam_init):
    batch, seq, _ = proj_b.shape
    pairs = B_HEADS // 2
    n_tiles = seq // BQ
    return pl.pallas_call(
        functools.partial(_attn_b_kernel, lam_init=lam_init),
        grid=(pairs, batch, n_tiles),
        in_specs=[pl.BlockSpec((1, BQ, LANES), lambda hp, b, i: (b, i, hp)),
                  pl.BlockSpec((1, seq, LANES), lambda hp, b, i: (b, 0, pairs + hp)),
                  pl.BlockSpec((1, seq, LANES), lambda hp, b, i: (b, 0, 2 * pairs + hp)),
                  pl.BlockSpec((2, n_tiles, BQ, BK), lambda hp, b, i: (hp, 0, 0, 0)),
                  pl.BlockSpec(diff_lambda.shape, lambda hp, b, i: (0, 0)),
                  pl.BlockSpec(subln_g2.shape, lambda hp, b, i: (0, 0))],
        out_specs=pl.BlockSpec((1, BQ, LANES), lambda hp, b, i: (b, i, hp)),
        out_shape=jax.ShapeDtypeStruct((batch, seq, B_HEADS * HEAD_DIM), BF16),
        compiler_params=pltpu.CompilerParams(dimension_semantics=("arbitrary",) * 3,
                                             vmem_limit_bytes=VMEM_LIMIT),
        name="attention_b",
    )(proj_b, proj_b, proj_b, bias_b, diff_lambda, subln_g2)


def _attn_c_kernel(q_ref, k_ref, v_ref, o_ref):
    qi = pl.program_id(2)
    lane = lax.broadcasted_iota(jnp.int32, (BQ, LANES), 1)
    ones = jnp.ones((BK, LANES), BF16)
    scale = (C_NOPE + C_ROPE) ** -0.5
    row = lax.broadcasted_iota(jnp.int32, (BQ, BK), 0)
    colm = lax.broadcasted_iota(jnp.int32, (BQ, BK), 1)
    diag_bias = jnp.where(row >= colm, 0.0, MASK_VALUE)

    outs = []
    for j in range(2):
        hs = slice(j * LANES, (j + 1) * LANES)
        q = q_ref[0, :, hs]

        def step(ki, carry, masked, q=q, hs=hs):
            m, acc = carry
            off = pl.multiple_of(ki * BK, BK)
            k = k_ref[0, pl.ds(off, BK), hs]
            v_aug = jnp.concatenate([v_ref[0, pl.ds(off, BK), :], ones], axis=1)
            s = _nt_dot(q, k) * scale
            if masked:
                s = s + diag_bias
            return _softmax_step(s, v_aug, m, acc)

        m0 = jnp.full((BQ, 1), MASK_VALUE, F32)
        a0 = jnp.zeros((BQ, 2 * LANES), F32)
        carry = lax.fori_loop(0, qi, functools.partial(step, masked=False), (m0, a0))
        _, acc = step(qi, carry, True)
        outs.append(acc[:, :LANES] / acc[:, LANES:])
    o_ref[0] = jnp.where(lane < HEAD_DIM, outs[0], outs[1]).astype(o_ref.dtype)


def _attention_c(qc, kc, vc):
    batch, seq, _ = qc.shape
    pairs = C_HEADS // 2
    n_tiles = seq // BQ
    return pl.pallas_call(
        _attn_c_kernel,
        grid=(pairs, batch, n_tiles),
        in_specs=[pl.BlockSpec((1, BQ, 2 * LANES), lambda hp, b, i: (b, i, hp)),
                  pl.BlockSpec((1, seq, 2 * LANES), lambda hp, b, i: (b, 0, hp)),
                  pl.BlockSpec((1, seq, LANES), lambda hp, b, i: (b, 0, hp))],
        out_specs=pl.BlockSpec((1, BQ, LANES), lambda hp, b, i: (b, i, hp)),
        out_shape=jax.ShapeDtypeStruct((batch, seq, C_HEADS * HEAD_DIM), BF16),
        compiler_params=pltpu.CompilerParams(dimension_semantics=("arbitrary",) * 3,
                                             vmem_limit_bytes=VMEM_LIMIT),
        name="attention_c",
    )(qc, kc, vc)


def _layer_norm(y, g, b):
    mu = jnp.mean(y, axis=-1, keepdims=True)
    yc = y - mu
    var = jnp.mean(yc * yc, axis=-1, keepdims=True)
    return yc * lax.rsqrt(var + LN_EPS) * g + b


def _out_ffn_kernel(x_ref, oa_ref, ob_ref, oc_ref, wo_ref, g1_ref, b1_ref, g2_ref, b2_ref,
                    wg_ref, wu_ref, wd_ref, out_ref, *, alpha):
    heads = jnp.concatenate([oa_ref[...], ob_ref[...], oc_ref[...]], axis=1)
    mix = jnp.dot(heads, wo_ref[...], preferred_element_type=F32)
    h = _layer_norm(alpha * x_ref[...] + mix, g1_ref[...], b1_ref[...])
    hb = h.astype(BF16)
    ffn = jnp.zeros(h.shape, F32)
    for c in range(FF_DIM // FFN_CHUNK):
        cs = slice(c * FFN_CHUNK, (c + 1) * FFN_CHUNK)
        gate = jnp.dot(hb, wg_ref[:, cs], preferred_element_type=F32)
        up = jnp.dot(hb, wu_ref[:, cs], preferred_element_type=F32)
        act = (jax.nn.silu(gate) * up).astype(BF16)
        ffn = ffn + jnp.dot(act, wd_ref[cs, :], preferred_element_type=F32)
    out_ref[...] = _layer_norm(alpha * h + ffn, g2_ref[...], b2_ref[...])


def _out_ffn(x2d, oa, ob, oc, wo, g1, b1, g2, b2, wg, wu, wd, alpha):
    n_tok = x2d.shape[0]
    tm = FFN_TM
    tok = lambda width: pl.BlockSpec((tm, width), lambda i: (i, 0))
    once = lambda a: pl.BlockSpec(a.shape, lambda i: (0, 0), pipeline_mode=pl.Buffered(1))
    return pl.pallas_call(
        functools.partial(_out_ffn_kernel, alpha=alpha),
        grid=(n_tok // tm,),
        in_specs=[tok(D_MODEL), tok(oa.shape[1]), tok(ob.shape[1]), tok(oc.shape[1]),
                  once(wo), once(g1), once(b1), once(g2), once(b2), once(wg), once(wu), once(wd)],
        out_specs=tok(D_MODEL),
        out_shape=jax.ShapeDtypeStruct((n_tok, D_MODEL), F32),
        compiler_params=pltpu.CompilerParams(dimension_semantics=("arbitrary",),
                                             vmem_limit_bytes=VMEM_LIMIT),
        name="out_ffn",
    )(x2d, oa, ob, oc, wo, g1, b1, g2, b2, wg, wu, wd)


def _rotate_half_cols(w):
    half = w.shape[-1] // 2
    return jnp.concatenate([-w[..., half:], w[..., :half]], axis=-1)


def _place_rope(w):
    return jnp.pad(w, ((0, 0), (C_NOPE, LANES - C_NOPE - C_ROPE)))


def _layer_weights(w_in, w_uq, w_ukv):
    main = A_COLS + B_COLS + C_Q_LORA + C_KV_LORA
    k_r = w_in[:, main:]
    w1 = jnp.concatenate([w_in[:, :main], _place_rope(k_r), _place_rope(_rotate_half_cols(k_r))],
                         axis=1).astype(BF16)
    rows = w_uq.shape[0]
    uq = w_uq.reshape(rows, C_HEADS, C_NOPE + C_ROPE)
    nope, rope = uq[..., :C_NOPE], uq[..., C_NOPE:]
    tail = jnp.zeros((rows, C_HEADS, LANES - C_NOPE - C_ROPE), w_uq.dtype)
    q_main = jnp.concatenate([nope, rope, tail], axis=-1).reshape(rows, C_HEADS * LANES)
    q_half = jnp.concatenate([jnp.zeros_like(nope), _rotate_half_cols(rope), tail],
                             axis=-1).reshape(rows, C_HEADS * LANES)
    wq = jnp.concatenate([q_main, q_half], axis=1).astype(BF16)
    rows = w_ukv.shape[0]
    ukv = w_ukv.reshape(rows, C_HEADS, C_NOPE + HEAD_DIM)
    k_nope, v = ukv[..., :C_NOPE], ukv[..., C_NOPE:]
    k_main = jnp.concatenate([k_nope, jnp.zeros((rows, C_HEADS, LANES - C_NOPE), w_ukv.dtype)],
                             axis=-1).reshape(rows, C_HEADS * LANES)
    wkv = jnp.concatenate([k_main, v.reshape(rows, C_HEADS * HEAD_DIM)], axis=1).astype(BF16)
    return w1, wq, wkv


def _rope_tables(seq):
    half = C_ROPE // 2
    inv = ROPE_THETA ** (-np.arange(half, dtype=np.float64) / half)
    ang = np.arange(seq, dtype=np.float64)[:, None] * inv[None, :]
    cos = np.zeros((seq, LANES), np.float64)
    sin = np.zeros((seq, LANES), np.float64)
    cos[:, :C_NOPE] = 1.0
    cos[:, C_NOPE:C_NOPE + C_ROPE] = np.concatenate([np.cos(ang), np.cos(ang)], axis=1)
    sin[:, C_NOPE:C_NOPE + C_ROPE] = np.concatenate([np.sin(ang), np.sin(ang)], axis=1)
    return jnp.asarray(cos, F32), jnp.asarray(sin, F32)


def kernel(x, rel_bias, w_in, q_norm_g, kv_norm_g, w_uq, w_ukv, diff_lambda, subln_g, w_o, ln1_g, ln1_b, ln2_g, ln2_b, w_gate, w_up, w_down):
    batch, seq, d_model = x.shape
    depth = w_in.shape[0]
    alpha = (2 * depth) ** 0.25
    bias_a = _bias_tiles(rel_bias, A_HEADS, 0, len(A_PAIRS), A_BLK, 2 * A_BLK, True)
    bias_b = _bias_tiles(rel_bias, B_HEADS, A_HEADS, seq // BQ, BQ, BK, False)
    cos_t, sin_t = _rope_tables(seq)
    row = lambda a: a.reshape(1, -1)
    x2d = x.reshape(batch * seq, d_model)
    for l in range(depth):
        w1, wq, wkv = _layer_weights(w_in[l], w_uq[l], w_ukv[l])
        pa, pb, qc, kc, vc = _projection(x2d, w1, row(q_norm_g[l]), row(kv_norm_g[l]), wq, wkv,
                                         cos_t, sin_t, seq)
        lam_init = 0.8 - 0.6 * math.exp(-0.3 * l)
        oa = _attention_a(pa.reshape(batch, seq, -1), bias_a)
        ob = _attention_b(pb.reshape(batch, seq, -1), bias_b, diff_lambda[l],
                          row(jnp.concatenate([subln_g[l], subln_g[l]])), lam_init)
        oc = _attention_c(qc.reshape(batch, seq, -1), kc.reshape(batch, seq, -1),
                          vc.reshape(batch, seq, -1))
        flat = lambda a: a.reshape(batch * seq, -1)
        x2d = _out_ffn(x2d, flat(oa), flat(ob), flat(oc), w_o[l].astype(BF16),
                       row(ln1_g[l]), row(ln1_b[l]), row(ln2_g[l]), row(ln2_b[l]),
                       w_gate[l].astype(BF16), w_up[l].astype(BF16), w_down[l].astype(BF16), alpha)
    return x2d.reshape(batch, seq, d_model)
```

```python
import functools
import math

import numpy as np
import jax
import jax.numpy as jnp
from jax import lax
from jax.experimental import pallas as pl
from jax.experimental.pallas import tpu as pltpu

F32 = jnp.float32
BF16 = jnp.bfloat16

D_MODEL = 1024
HEAD_DIM = 64
A_HEADS = 6
A_PAIRS = ((128, 1), (512, 4), (2048, 16))
A_BLK = 128
A_UNROLL = (5, 1, 4)
B_HEADS = 4
B_QK_DIM = 32
C_HEADS = 6
C_Q_LORA = 256
C_KV_LORA = 128
C_NOPE = 64
C_ROPE = 32
ROPE_THETA = 10000.0
A_COLS = 3 * A_HEADS * HEAD_DIM
B_COLS = B_HEADS * 3 * HEAD_DIM
NUM_BUCKETS = 32
MAX_DISTANCE = 2048
FF_DIM = 2816
LN_EPS = 1e-5
LATENT_EPS = 1e-6
SUBLN_EPS = 1e-5

LANES = 128
MASK_VALUE = -1e30
VMEM_LIMIT = 56 * 1024 * 1024

PROJ_TM = 512
FFN_TM = 512
FFN_CHUNK = 1408
BQ = 512
BK = 512


def _bucket_lower_bounds():
    max_exact = NUM_BUCKETS // 2
    d = np.arange(0, MAX_DISTANCE + 1)
    val = np.log(np.maximum(d, 1) / max_exact) / math.log(MAX_DISTANCE / max_exact) * (NUM_BUCKETS - max_exact)
    large = np.minimum(max_exact + np.floor(np.maximum(val, 0.0)).astype(np.int64), NUM_BUCKETS - 1)
    bucket = np.where(d < max_exact, d, large)
    return [int(np.argmax(bucket >= b)) for b in range(NUM_BUCKETS)]


_BUCKET_LO = _bucket_lower_bounds()


def _nt_dot(a, b):
    return lax.dot_general(a, b, (((1,), (1,)), ((), ())), preferred_element_type=F32)


def _bias_tile_kernel(rb_ref, out_ref, *, head_offset, dilated):
    h = pl.program_id(0) + head_offset
    g = pl.program_id(1)
    rows, cols = out_ref.shape[2], out_ref.shape[3]
    i = lax.broadcasted_iota(jnp.int32, (rows, cols), 0)
    j = lax.broadcasted_iota(jnp.int32, (rows, cols), 1)
    if dilated:
        rel = i + A_BLK - j
        valid = (rel >= 0) & (rel <= A_BLK)
        dil = jnp.where(g == 0, A_PAIRS[0][1], jnp.where(g == 1, A_PAIRS[1][1], A_PAIRS[2][1]))
        dist = rel * dil
    else:
        dist = g * rows + i - j
        valid = dist >= 0
    val = jnp.full((rows, cols), rb_ref[0, h], F32)
    for b in range(1, NUM_BUCKETS):
        val = jnp.where(dist >= _BUCKET_LO[b], rb_ref[b, h], val)
    out_ref[0, 0] = jnp.where(valid, val, MASK_VALUE)


def _bias_tiles(rel_bias, n_heads, head_offset, n_groups, rows, cols, dilated):
    return pl.pallas_call(
        functools.partial(_bias_tile_kernel, head_offset=head_offset, dilated=dilated),
        grid=(n_heads, n_groups),
        in_specs=[pl.BlockSpec(memory_space=pltpu.SMEM)],
        out_specs=pl.BlockSpec((1, 1, rows, cols), lambda h, g: (h, g, 0, 0)),
        out_shape=jax.ShapeDtypeStruct((n_heads, n_groups, rows, cols), F32),
        name="bias_tiles",
    )(rel_bias)


def _proj_kernel(x_ref, w1_ref, qg_ref, kvg_ref, wq_ref, wkv_ref, cos_ref, sin_ref,
                 pa_ref, pb_ref, qc_ref, kc_ref, vc_ref):
    xb = x_ref[...].astype(BF16)
    pa_ref[...] = jnp.dot(xb, w1_ref[:, :A_COLS], preferred_element_type=F32)
    pb_ref[...] = jnp.dot(xb, w1_ref[:, A_COLS:A_COLS + B_COLS], preferred_element_type=F32).astype(BF16)
    c = jnp.dot(xb, w1_ref[:, A_COLS + B_COLS:], preferred_element_type=F32)
    c_q = c[:, :C_Q_LORA]
    c_kv = c[:, C_Q_LORA:C_Q_LORA + C_KV_LORA]
    k_rot = c[:, C_Q_LORA + C_KV_LORA:C_Q_LORA + C_KV_LORA + LANES]
    k_rot_half = c[:, C_Q_LORA + C_KV_LORA + LANES:]
    cqn = (c_q * lax.rsqrt(jnp.mean(c_q * c_q, axis=-1, keepdims=True) + LATENT_EPS) * qg_ref[...]).astype(BF16)
    ckvn = (c_kv * lax.rsqrt(jnp.mean(c_kv * c_kv, axis=-1, keepdims=True) + LATENT_EPS) * kvg_ref[...]).astype(BF16)
    q2 = jnp.dot(cqn, wq_ref[...], preferred_element_type=F32)
    kv = jnp.dot(ckvn, wkv_ref[...], preferred_element_type=F32)
    cos = cos_ref[...]
    sin = sin_ref[...]
    k_rope = k_rot * cos + k_rot_half * sin
    qw = C_HEADS * LANES
    for h in range(C_HEADS):
        sl = slice(h * LANES, (h + 1) * LANES)
        sl2 = slice(qw + h * LANES, qw + (h + 1) * LANES)
        qc_ref[:, sl] = (q2[:, sl] * cos + q2[:, sl2] * sin).astype(BF16)
        kc_ref[:, sl] = (kv[:, sl] + k_rope).astype(BF16)
    vc_ref[...] = kv[:, qw:].astype(BF16)


def _projection(x2d, w1, qg, kvg, wq, wkv, cos_t, sin_t, seq):
    n_tok = x2d.shape[0]
    tm = PROJ_TM
    pos_blocks = seq // tm
    full = lambda shape: pl.BlockSpec(shape, lambda i: (0, 0))
    tok = lambda width: pl.BlockSpec((tm, width), lambda i: (i, 0))
    pos = pl.BlockSpec((tm, LANES), lambda i: (i % pos_blocks, 0))
    return pl.pallas_call(
        _proj_kernel,
        grid=(n_tok // tm,),
        in_specs=[tok(D_MODEL), full(w1.shape), full(qg.shape), full(kvg.shape), full(wq.shape),
                  full(wkv.shape), pos, pos],
        out_specs=[tok(A_COLS), tok(B_COLS), tok(C_HEADS * LANES), tok(C_HEADS * LANES),
                   tok(C_HEADS * HEAD_DIM)],
        out_shape=[jax.ShapeDtypeStruct((n_tok, A_COLS), F32),
                   jax.ShapeDtypeStruct((n_tok, B_COLS), BF16),
                   jax.ShapeDtypeStruct((n_tok, C_HEADS * LANES), BF16),
                   jax.ShapeDtypeStruct((n_tok, C_HEADS * LANES), BF16),
                   jax.ShapeDtypeStruct((n_tok, C_HEADS * HEAD_DIM), BF16)],
        compiler_params=pltpu.CompilerParams(dimension_semantics=("arbitrary",),
                                             vmem_limit_bytes=VMEM_LIMIT),
        name="projection",
    )(x2d, w1, qg, kvg, wq, wkv, cos_t, sin_t)


def _attn_a_kernel(q_ref, k_ref, v_ref, bias_ref, o_ref, os_ref, ls_ref, ms_ref):
    seq = q_ref.shape[1]
    lane = lax.broadcasted_iota(jnp.int32, (A_BLK, LANES), 1)
    lo_half = lane < HEAD_DIM
    ones = jnp.ones((2 * A_BLK, LANES), BF16)
    scale = HEAD_DIM ** -0.5

    def rows(start, size, dil):
        if dil == 1:
            return pl.ds(start, size)
        return pl.ds(start, size, stride=dil)

    def block(p, dil, q_start, first):
        qf = q_ref[0, rows(q_start, A_BLK, dil), :]
        if first:
            k_start, nk = q_start, A_BLK
        else:
            k_start, nk = q_start - A_BLK * dil, 2 * A_BLK
        kb = k_ref[0, rows(k_start, nk, dil), :].astype(BF16)
        vb = v_ref[0, rows(k_start, nk, dil), :].astype(BF16)
        v_aug = jnp.concatenate([vb, ones[:nk]], axis=1)
        acc, den, mx = [], [], []
        for j in range(2):
            qj = jnp.where(lo_half if j == 0 else jnp.logical_not(lo_half), qf, 0.0).astype(BF16)
            bias = bias_ref[j, p]
            if first:
                bias = bias[:, A_BLK:]
            s = _nt_dot(qj, kb) * scale + bias
            m = jnp.max(s, axis=-1, keepdims=True)
            pr = jnp.exp(s - m).astype(BF16)
            r = jnp.dot(pr, v_aug, preferred_element_type=F32)
            acc.append(r[:, :LANES])
            den.append(r[:, LANES:])
            mx.append(jnp.broadcast_to(m, (A_BLK, LANES)))
        dst = rows(q_start, A_BLK, dil)
        os_ref[p, dst, :] = jnp.where(lo_half, acc[0], acc[1])
        ls_ref[p, dst, :] = jnp.where(lo_half, den[0], den[1])
        ms_ref[p, dst, :] = jnp.where(lo_half, mx[0], mx[1])

    for p, (window, dil) in enumerate(A_PAIRS):
        n_blocks = seq // dil // A_BLK
        if dil == 1:
            block(p, dil, 0, True)

            def later(n, c, p=p):
                block(p, 1, pl.multiple_of(n * A_BLK, A_BLK), False)
                return c
            lax.fori_loop(1, n_blocks, later, 0, unroll=A_UNROLL[p])
        else:
            def residue_class(r, c, p=p, dil=dil, n_blocks=n_blocks):
                block(p, dil, r, True)
                for n in range(1, n_blocks):
                    block(p, dil, n * (A_BLK * dil) + r, False)
                return c
            lax.fori_loop(0, dil, residue_class, 0, unroll=A_UNROLL[p])

    chunk = 256

    def combine(c, carry):
        rs = pl.ds(pl.multiple_of(c * chunk, chunk), chunk)
        m0, m1, m2 = ms_ref[0, rs, :], ms_ref[1, rs, :], ms_ref[2, rs, :]
        mx = jnp.maximum(jnp.maximum(m0, m1), m2)
        e0, e1, e2 = jnp.exp(m0 - mx), jnp.exp(m1 - mx), jnp.exp(m2 - mx)
        num = e0 * os_ref[0, rs, :] + e1 * os_ref[1, rs, :] + e2 * os_ref[2, rs, :]
        den = e0 * ls_ref[0, rs, :] + e1 * ls_ref[1, rs, :] + e2 * ls_ref[2, rs, :]
        o_ref[0, rs, :] = (num / den).astype(o_ref.dtype)
        return carry

    lax.fori_loop(0, seq // chunk, combine, 0)


def _attention_a(proj_a, bias_a):
    batch, seq, _ = proj_a.shape
    pairs = A_HEADS // 2
    n_pairs = len(A_PAIRS)
    col = lambda off: pl.BlockSpec((1, seq, LANES), lambda hp, b: (b, 0, off + hp))
    return pl.pallas_call(
        _attn_a_kernel,
        grid=(pairs, batch),
        in_specs=[col(0), col(pairs), col(2 * pairs),
                  pl.BlockSpec((2, n_pairs, A_BLK, 2 * A_BLK), lambda hp, b: (hp, 0, 0, 0))],
        out_specs=pl.BlockSpec((1, seq, LANES), lambda hp, b: (b, 0, hp)),
        out_shape=jax.ShapeDtypeStruct((batch, seq, A_HEADS * HEAD_DIM), BF16),
        scratch_shapes=[pltpu.VMEM((n_pairs, seq, LANES), F32)] * 3,
        compiler_params=pltpu.CompilerParams(dimension_semantics=("arbitrary", "arbitrary"),
                                             vmem_limit_bytes=VMEM_LIMIT),
        name="attention_a",
    )(proj_a, proj_a, proj_a, bias_a)


def _softmax_step(s, v_aug, m, acc):
    m_new = jnp.maximum(m, jnp.max(s, axis=-1, keepdims=True))
    alpha = jnp.exp(m - m_new)
    pr = jnp.exp(s - m_new).astype(BF16)
    return m_new, acc * alpha + jnp.dot(pr, v_aug, preferred_element_type=F32)


def _attn_b_kernel(q_ref, k_ref, v_ref, bias_ref, lam_ref, g_ref, o_ref, *, lam_init):
    qi = pl.program_id(2)
    q = q_ref[0]
    lane = lax.broadcasted_iota(jnp.int32, (BQ, LANES), 1)
    ones = jnp.ones((BK, LANES), BF16)
    scale = B_QK_DIM ** -0.5
    lf = lam_ref[...]
    dot_rows = lambda a, b: jnp.sum(lf[a:a + 1] * lf[b:b + 1], axis=-1, keepdims=True)
    lam = jnp.exp(dot_rows(0, 1)) - jnp.exp(dot_rows(2, 3)) + lam_init

    qs = [jnp.where((lane >= c * B_QK_DIM) & (lane < (c + 1) * B_QK_DIM), q, jnp.zeros_like(q))
          for c in range(4)]

    def step(ki, carry):
        off = pl.multiple_of(ki * BK, BK)
        k = k_ref[0, pl.ds(off, BK), :]
        v_aug = jnp.concatenate([v_ref[0, pl.ds(off, BK), :], ones], axis=1)
        out = []
        for c in range(4):
            bias = bias_ref[c // 2, qi - ki]
            out.append(_softmax_step(_nt_dot(qs[c], k) * scale + bias, v_aug, *carry[c]))
        return tuple(out)

    m0 = jnp.full((BQ, 1), MASK_VALUE, F32)
    a0 = jnp.zeros((BQ, 2 * LANES), F32)
    chains = lax.fori_loop(0, qi + 1, step, ((m0, a0),) * 4)

    outs = []
    for j in range(2):
        base = j * HEAD_DIM
        a1, a2 = chains[2 * j][1], chains[2 * j + 1][1]
        o = a1[:, :LANES] / a1[:, LANES:] - lam * (a2[:, :LANES] / a2[:, LANES:])
        in_head = (lane >= base) & (lane < base + HEAD_DIM)
        ms = jnp.sum(jnp.where(in_head, o * o, 0.0), axis=-1, keepdims=True) * (1.0 / HEAD_DIM)
        outs.append(o * lax.rsqrt(ms + SUBLN_EPS) * g_ref[...] * (1.0 - lam_init))
    o_ref[0] = jnp.where(lane < HEAD_DIM, outs[0], outs[1]).astype(o_ref.dtype)


def _attention_b(proj_b, bias_b, diff_lambda, subln_g2, lam_init):
    batch, seq, _ = proj_b.shape
    pairs = B_HEADS // 2
    n_tiles = seq // BQ
    return pl.pallas_call(
        functools.partial(_attn_b_kernel, lam_init=lam_init),
        grid=(pairs, batch, n_tiles),
        in_specs=[pl.BlockSpec((1, BQ, LANES), lambda hp, b, i: (b, i, hp)),
                  pl.BlockSpec((1, seq, LANES), lambda hp, b, i: (b, 0, pairs + hp)),
                  pl.BlockSpec((1, seq, LANES), lambda hp, b, i: (b, 0, 2 * pairs + hp)),
                  pl.BlockSpec((2, n_tiles, BQ, BK), lambda hp, b, i: (hp, 0, 0, 0),
                               pipeline_mode=pl.Buffered(1)),
                  pl.BlockSpec(diff_lambda.shape, lambda hp, b, i: (0, 0)),
                  pl.BlockSpec(subln_g2.shape, lambda hp, b, i: (0, 0))],
        out_specs=pl.BlockSpec((1, BQ, LANES), lambda hp, b, i: (b, i, hp)),
        out_shape=jax.ShapeDtypeStruct((batch, seq, B_HEADS * HEAD_DIM), BF16),
        compiler_params=pltpu.CompilerParams(dimension_semantics=("arbitrary",) * 3,
                                             vmem_limit_bytes=VMEM_LIMIT),
        name="attention_b",
    )(proj_b, proj_b, proj_b, bias_b, diff_lambda, subln_g2)


def _attn_c_kernel(q_ref, k_ref, v_ref, o_ref):
    qi = pl.program_id(2)
    lane = lax.broadcasted_iota(jnp.int32, (BQ, LANES), 1)
    ones = jnp.ones((BK, LANES), BF16)
    scale = (C_NOPE + C_ROPE) ** -0.5
    row = lax.broadcasted_iota(jnp.int32, (BQ, BK), 0)
    colm = lax.broadcasted_iota(jnp.int32, (BQ, BK), 1)
    diag_bias = jnp.where(row >= colm, 0.0, MASK_VALUE)

    def step(ki, carry, masked):
        off = pl.multiple_of(ki * BK, BK)
        v_aug = jnp.concatenate([v_ref[0, pl.ds(off, BK), :], ones], axis=1)
        out = []
        for j in range(2):
            hs = slice(j * LANES, (j + 1) * LANES)
            s = _nt_dot(q_ref[0, :, hs], k_ref[0, pl.ds(off, BK), hs]) * scale
            if masked:
                s = s + diag_bias
            out.append(_softmax_step(s, v_aug, *carry[j]))
        return tuple(out)

    m0 = jnp.full((BQ, 1), MASK_VALUE, F32)
    a0 = jnp.zeros((BQ, 2 * LANES), F32)
    carry = lax.fori_loop(0, qi, functools.partial(step, masked=False), ((m0, a0),) * 2)
    (_, acc0), (_, acc1) = step(qi, carry, True)
    o_ref[0] = jnp.where(lane < HEAD_DIM, acc0[:, :LANES] / acc0[:, LANES:],
                         acc1[:, :LANES] / acc1[:, LANES:]).astype(o_ref.dtype)


def _attention_c(qc, kc, vc):
    batch, seq, _ = qc.shape
    pairs = C_HEADS // 2
    n_tiles = seq // BQ
    return pl.pallas_call(
        _attn_c_kernel,
        grid=(pairs, batch, n_tiles),
        in_specs=[pl.BlockSpec((1, BQ, 2 * LANES), lambda hp, b, i: (b, i, hp)),
                  pl.BlockSpec((1, seq, 2 * LANES), lambda hp, b, i: (b, 0, hp)),
                  pl.BlockSpec((1, seq, LANES), lambda hp, b, i: (b, 0, hp))],
        out_specs=pl.BlockSpec((1, BQ, LANES), lambda hp, b, i: (b, i, hp)),
        out_shape=jax.ShapeDtypeStruct((batch, seq, C_HEADS * HEAD_DIM), BF16),
        compiler_params=pltpu.CompilerParams(dimension_semantics=("arbitrary",) * 3,
                                             vmem_limit_bytes=VMEM_LIMIT),
        name="attention_c",
    )(qc, kc, vc)


def _layer_norm(y, g, b):
    mu = jnp.mean(y, axis=-1, keepdims=True)
    yc = y - mu
    var = jnp.mean(yc * yc, axis=-1, keepdims=True)
    return yc * lax.rsqrt(var + LN_EPS) * g + b


def _out_ffn_kernel(x_ref, oa_ref, ob_ref, oc_ref, wo_ref, g1_ref, b1_ref, g2_ref, b2_ref,
                    wg_ref, wu_ref, wd_ref, out_ref, *, alpha):
    heads = jnp.concatenate([oa_ref[...], ob_ref[...], oc_ref[...]], axis=1)
    mix = jnp.dot(heads, wo_ref[...], preferred_element_type=F32)
    h = _layer_norm(alpha * x_ref[...] + mix, g1_ref[...], b1_ref[...])
    hb = h.astype(BF16)
    ffn = jnp.zeros(h.shape, F32)
    for c in range(FF_DIM // FFN_CHUNK):
        cs = slice(c * FFN_CHUNK, (c + 1) * FFN_CHUNK)
        gate = jnp.dot(hb, wg_ref[:, cs], preferred_element_type=F32)
        up = jnp.dot(hb, wu_ref[:, cs], preferred_element_type=F32)
        act = (jax.nn.silu(gate) * up).astype(BF16)
        ffn = ffn + jnp.dot(act, wd_ref[cs, :], preferred_element_type=F32)
    out_ref[...] = _layer_norm(alpha * h + ffn, g2_ref[...], b2_ref[...])


def _out_ffn(x2d, oa, ob, oc, wo, g1, b1, g2, b2, wg, wu, wd, alpha):
    n_tok = x2d.shape[0]
    tm = FFN_TM
    tok = lambda width: pl.BlockSpec((tm, width), lambda i: (i, 0))
    once = lambda a: pl.BlockSpec(a.shape, lambda i: (0, 0), pipeline_mode=pl.Buffered(1))
    return pl.pallas_call(
        functools.partial(_out_ffn_kernel, alpha=alpha),
        grid=(n_tok // tm,),
        in_specs=[tok(D_MODEL), tok(oa.shape[1]), tok(ob.shape[1]), tok(oc.shape[1]),
                  once(wo), once(g1), once(b1), once(g2), once(b2), once(wg), once(wu), once(wd)],
        out_specs=tok(D_MODEL),
        out_shape=jax.ShapeDtypeStruct((n_tok, D_MODEL), F32),
        compiler_params=pltpu.CompilerParams(dimension_semantics=("arbitrary",),
                                             vmem_limit_bytes=VMEM_LIMIT),
        name="out_ffn",
    )(x2d, oa, ob, oc, wo, g1, b1, g2, b2, wg, wu, wd)


def _rotate_half_cols(w):
    half = w.shape[-1] // 2
    return jnp.concatenate([-w[..., half:], w[..., :half]], axis=-1)


def _place_rope(w):
    return jnp.pad(w, ((0, 0), (C_NOPE, LANES - C_NOPE - C_ROPE)))


def _layer_weights(w_in, w_uq, w_ukv):
    main = A_COLS + B_COLS + C_Q_LORA + C_KV_LORA
    k_r = w_in[:, main:]
    w1 = jnp.concatenate([w_in[:, :main], _place_rope(k_r), _place_rope(_rotate_half_cols(k_r))],
                         axis=1).astype(BF16)
    rows = w_uq.shape[0]
    uq = w_uq.reshape(rows, C_HEADS, C_NOPE + C_ROPE)
    nope, rope = uq[..., :C_NOPE], uq[..., C_NOPE:]
    tail = jnp.zeros((rows, C_HEADS, LANES - C_NOPE - C_ROPE), w_uq.dtype)
    q_main = jnp.concatenate([nope, rope, tail], axis=-1).reshape(rows, C_HEADS * LANES)
    q_half = jnp.concatenate([jnp.zeros_like(nope), _rotate_half_cols(rope), tail],
                             axis=-1).reshape(rows, C_HEADS * LANES)
    wq = jnp.concatenate([q_main, q_half], axis=1).astype(BF16)
    rows = w_ukv.shape[0]
    ukv = w_ukv.reshape(rows, C_HEADS, C_NOPE + HEAD_DIM)
    k_nope, v = ukv[..., :C_NOPE], ukv[..., C_NOPE:]
    k_main = jnp.concatenate([k_nope, jnp.zeros((rows, C_HEADS, LANES - C_NOPE), w_ukv.dtype)],
                             axis=-1).reshape(rows, C_HEADS * LANES)
    wkv = jnp.concatenate([k_main, v.reshape(rows, C_HEADS * HEAD_DIM)], axis=1).astype(BF16)
    return w1, wq, wkv


def _rope_tables(seq):
    half = C_ROPE // 2
    inv = ROPE_THETA ** (-np.arange(half, dtype=np.float64) / half)
    ang = np.arange(seq, dtype=np.float64)[:, None] * inv[None, :]
    cos = np.zeros((seq, LANES), np.float64)
    sin = np.zeros((seq, LANES), np.float64)
    cos[:, :C_NOPE] = 1.0
    cos[:, C_NOPE:C_NOPE + C_ROPE] = np.concatenate([np.cos(ang), np.cos(ang)], axis=1)
    sin[:, C_NOPE:C_NOPE + C_ROPE] = np.concatenate([np.sin(ang), np.sin(ang)], axis=1)
    return jnp.asarray(cos, F32), jnp.asarray(sin, F32)


def kernel(x, rel_bias, w_in, q_norm_g, kv_norm_g, w_uq, w_ukv, diff_lambda, subln_g, w_o, ln1_g, ln1_b, ln2_g, ln2_b, w_gate, w_up, w_down):
    batch, seq, d_model = x.shape
    depth = w_in.shape[0]
    alpha = (2 * depth) ** 0.25
    bias_a = _bias_tiles(rel_bias, A_HEADS, 0, len(A_PAIRS), A_BLK, 2 * A_BLK, True)
    bias_b = _bias_tiles(rel_bias, B_HEADS, A_HEADS, seq // BQ, BQ, BK, False)
    cos_t, sin_t = _rope_tables(seq)
    row = lambda a: a.reshape(1, -1)
    x2d = x.reshape(batch * seq, d_model)
    for l in range(depth):
        w1, wq, wkv = _layer_weights(w_in[l], w_uq[l], w_ukv[l])
        pa, pb, qc, kc, vc = _projection(x2d, w1, row(q_norm_g[l]), row(kv_norm_g[l]), wq, wkv,
                                         cos_t, sin_t, seq)
        lam_init = 0.8 - 0.6 * math.exp(-0.3 * l)
        oa = _attention_a(pa.reshape(batch, seq, -1), bias_a)
        ob = _attention_b(pb.reshape(batch, seq, -1), bias_b, diff_lambda[l],
                          row(jnp.concatenate([subln_g[l], subln_g[l]])), lam_init)
        oc = _attention_c(qc.reshape(batch, seq, -1), kc.reshape(batch, seq, -1),
                          vc.reshape(batch, seq, -1))
        flat = lambda a: a.reshape(batch * seq, -1)
        x2d = _out_ffn(x2d, flat(oa), flat(ob), flat(oc), w_o[l].astype(BF16),
                       row(ln1_g[l]), row(ln1_b[l]), row(ln2_g[l]), row(ln2_b[l]),
                       w_gate[l].astype(BF16), w_up[l].astype(BF16), w_down[l].astype(BF16), alpha)
    return x2d.reshape(batch, seq, d_model)
```

```python
import functools
import math

import numpy as np
import jax
import jax.numpy as jnp
from jax import lax
from jax.experimental import pallas as pl
from jax.experimental.pallas import tpu as pltpu

F32 = jnp.float32
BF16 = jnp.bfloat16

D_MODEL = 1024
HEAD_DIM = 64
A_HEADS = 6
A_PAIRS = ((128, 1), (512, 4), (2048, 16))
A_BLK = 128
A_UNROLL = (5, 1, 4)
B_HEADS = 4
B_QK_DIM = 32
C_HEADS = 6
C_Q_LORA = 256
C_KV_LORA = 128
C_NOPE = 64
C_ROPE = 32
ROPE_THETA = 10000.0
A_COLS = 3 * A_HEADS * HEAD_DIM
B_COLS = B_HEADS * 3 * HEAD_DIM
NUM_BUCKETS = 32
MAX_DISTANCE = 2048
FF_DIM = 2816
LN_EPS = 1e-5
LATENT_EPS = 1e-6
SUBLN_EPS = 1e-5

LANES = 128
MASK_VALUE = -1e30
VMEM_LIMIT = 56 * 1024 * 1024

PROJ_TM = 512
FFN_TM = 512
FFN_CHUNK = 1408
BQ = 512
BK = 512


def _bucket_lower_bounds():
    max_exact = NUM_BUCKETS // 2
    d = np.arange(0, MAX_DISTANCE + 1)
    val = np.log(np.maximum(d, 1) / max_exact) / math.log(MAX_DISTANCE / max_exact) * (NUM_BUCKETS - max_exact)
    large = np.minimum(max_exact + np.floor(np.maximum(val, 0.0)).astype(np.int64), NUM_BUCKETS - 1)
    bucket = np.where(d < max_exact, d, large)
    return [int(np.argmax(bucket >= b)) for b in range(NUM_BUCKETS)]


_BUCKET_LO = _bucket_lower_bounds()


def _nt_dot(a, b):
    return lax.dot_general(a, b, (((1,), (1,)), ((), ())), preferred_element_type=F32)


def _bias_tile_kernel(rb_ref, out_ref, *, head_offset, dilated):
    h = pl.program_id(0) + head_offset
    g = pl.program_id(1)
    rows, cols = out_ref.shape[2], out_ref.shape[3]
    i = lax.broadcasted_iota(jnp.int32, (rows, cols), 0)
    j = lax.broadcasted_iota(jnp.int32, (rows, cols), 1)
    if dilated:
        rel = i + A_BLK - j
        valid = (rel >= 0) & (rel <= A_BLK)
        dil = jnp.where(g == 0, A_PAIRS[0][1], jnp.where(g == 1, A_PAIRS[1][1], A_PAIRS[2][1]))
        dist = rel * dil
    else:
        dist = g * rows + i - j
        valid = dist >= 0
    val = jnp.full((rows, cols), rb_ref[0, h], F32)
    for b in range(1, NUM_BUCKETS):
        val = jnp.where(dist >= _BUCKET_LO[b], rb_ref[b, h], val)
    out_ref[0, 0] = jnp.where(valid, val, MASK_VALUE)


def _bias_tiles(rel_bias, n_heads, head_offset, n_groups, rows, cols, dilated):
    return pl.pallas_call(
        functools.partial(_bias_tile_kernel, head_offset=head_offset, dilated=dilated),
        grid=(n_heads, n_groups),
        in_specs=[pl.BlockSpec(memory_space=pltpu.SMEM)],
        out_specs=pl.BlockSpec((1, 1, rows, cols), lambda h, g: (h, g, 0, 0)),
        out_shape=jax.ShapeDtypeStruct((n_heads, n_groups, rows, cols), F32),
        name="bias_tiles",
    )(rel_bias)


def _proj_kernel(x_ref, w1_ref, qg_ref, kvg_ref, wq_ref, wkv_ref, cos_ref, sin_ref,
                 pa_ref, pb_ref, qc_ref, kc_ref, vc_ref):
    xb = x_ref[...].astype(BF16)
    pa_ref[...] = jnp.dot(xb, w1_ref[:, :A_COLS], preferred_element_type=F32)
    pb_ref[...] = jnp.dot(xb, w1_ref[:, A_COLS:A_COLS + B_COLS], preferred_element_type=F32).astype(BF16)
    c = jnp.dot(xb, w1_ref[:, A_COLS + B_COLS:], preferred_element_type=F32)
    c_q = c[:, :C_Q_LORA]
    c_kv = c[:, C_Q_LORA:C_Q_LORA + C_KV_LORA]
    k_rot = c[:, C_Q_LORA + C_KV_LORA:C_Q_LORA + C_KV_LORA + LANES]
    k_rot_half = c[:, C_Q_LORA + C_KV_LORA + LANES:]
    cqn = (c_q * lax.rsqrt(jnp.mean(c_q * c_q, axis=-1, keepdims=True) + LATENT_EPS) * qg_ref[...]).astype(BF16)
    ckvn = (c_kv * lax.rsqrt(jnp.mean(c_kv * c_kv, axis=-1, keepdims=True) + LATENT_EPS) * kvg_ref[...]).astype(BF16)
    q2 = jnp.dot(cqn, wq_ref[...], preferred_element_type=F32)
    kv = jnp.dot(ckvn, wkv_ref[...], preferred_element_type=F32)
    cos = cos_ref[...]
    sin = sin_ref[...]
    k_rope = k_rot * cos + k_rot_half * sin
    qw = C_HEADS * LANES
    for h in range(C_HEADS):
        sl = slice(h * LANES, (h + 1) * LANES)
        sl2 = slice(qw + h * LANES, qw + (h + 1) * LANES)
        qc_ref[:, sl] = (q2[:, sl] * cos + q2[:, sl2] * sin).astype(BF16)
        kc_ref[:, sl] = (kv[:, sl] + k_rope).astype(BF16)
    vc_ref[...] = kv[:, qw:].astype(BF16)


def _projection(x2d, w1, qg, kvg, wq, wkv, cos_t, sin_t, seq):
    n_tok = x2d.shape[0]
    tm = PROJ_TM
    pos_blocks = seq // tm
    full = lambda shape: pl.BlockSpec(shape, lambda i: (0, 0))
    tok = lambda width: pl.BlockSpec((tm, width), lambda i: (i, 0))
    pos = pl.BlockSpec((tm, LANES), lambda i: (i % pos_blocks, 0))
    return pl.pallas_call(
        _proj_kernel,
        grid=(n_tok // tm,),
        in_specs=[tok(D_MODEL), full(w1.shape), full(qg.shape), full(kvg.shape), full(wq.shape),
                  full(wkv.shape), pos, pos],
        out_specs=[tok(A_COLS), tok(B_COLS), tok(C_HEADS * LANES), tok(C_HEADS * LANES),
                   tok(C_HEADS * HEAD_DIM)],
        out_shape=[jax.ShapeDtypeStruct((n_tok, A_COLS), F32),
                   jax.ShapeDtypeStruct((n_tok, B_COLS), BF16),
                   jax.ShapeDtypeStruct((n_tok, C_HEADS * LANES), BF16),
                   jax.ShapeDtypeStruct((n_tok, C_HEADS * LANES), BF16),
                   jax.ShapeDtypeStruct((n_tok, C_HEADS * HEAD_DIM), BF16)],
        compiler_params=pltpu.CompilerParams(dimension_semantics=("arbitrary",),
                                             vmem_limit_bytes=VMEM_LIMIT),
        name="projection",
    )(x2d, w1, qg, kvg, wq, wkv, cos_t, sin_t)


def _attn_a_kernel(q_ref, k_ref, v_ref, bias_ref, o_ref, d4_ref, d16_ref, r0_ref, r1c_ref, r1_ref,
                   r2c_ref, r2m_ref, r2_ref):
    seq = q_ref.shape[1]
    len4, len16 = seq // 4, seq // 16
    lane = lax.broadcasted_iota(jnp.int32, (A_BLK, LANES), 1)
    lo_half = lane < HEAD_DIM
    ones = jnp.ones((2 * A_BLK, LANES), BF16)
    scale = HEAD_DIM ** -0.5

    for a, src in enumerate((q_ref, k_ref, v_ref)):
        for r4 in range(4):
            d4_ref[a, r4] = src[0, pl.ds(r4, len4, stride=4), :]
        for r4 in range(4):
            for hi in range(4):
                d16_ref[a, 4 * hi + r4] = d4_ref[a, r4, pl.ds(hi, len16, stride=4), :]

    def attend(qf, kf, vf, p, first):
        nk = kf.shape[0]
        kb = kf.astype(BF16)
        v_aug = jnp.concatenate([vf.astype(BF16), ones[:nk]], axis=1)
        out, lse = [], []
        for j in range(2):
            qj = jnp.where(lo_half if j == 0 else jnp.logical_not(lo_half), qf, 0.0).astype(BF16)
            bias = bias_ref[j, p]
            if first:
                bias = bias[:, A_BLK:]
            s = _nt_dot(qj, kb) * scale + bias
            m = jnp.max(s, axis=-1, keepdims=True)
            pr = jnp.exp(s - m).astype(BF16)
            r = jnp.dot(pr, v_aug, preferred_element_type=F32)
            den = r[:, LANES:]
            out.append(r[:, :LANES] / den)
            lse.append(m + jnp.log(den))
        return jnp.where(lo_half, out[0], out[1]), jnp.where(lo_half, lse[0], lse[1])

    def key_rows(start, first):
        return pl.ds(start, A_BLK) if first else pl.ds(start - A_BLK, 2 * A_BLK)

    def pair0_block(start, first):
        kr = key_rows(start, first)
        out, lse = attend(q_ref[0, pl.ds(start, A_BLK), :], k_ref[0, kr, :], v_ref[0, kr, :], 0, first)
        r0_ref[0, pl.ds(start, A_BLK), :] = out
        r0_ref[1, pl.ds(start, A_BLK), :] = lse

    pair0_block(0, True)

    def pair0_later(n, c):
        pair0_block(pl.multiple_of(n * A_BLK, A_BLK), False)
        return c
    lax.fori_loop(1, seq // A_BLK, pair0_later, 0, unroll=A_UNROLL[0])

    def pair1_class(r4, c):
        for n in range(len4 // A_BLK):
            kr = key_rows(n * A_BLK, n == 0)
            out, lse = attend(d4_ref[0, r4, pl.ds(n * A_BLK, A_BLK), :], d4_ref[1, r4, kr, :],
                              d4_ref[2, r4, kr, :], 1, n == 0)
            r1c_ref[0, r4, pl.ds(n * A_BLK, A_BLK), :] = out
            r1c_ref[1, r4, pl.ds(n * A_BLK, A_BLK), :] = lse
        return c
    for r4 in range(4):
        pair1_class(r4, 0)

    def pair2_class(r, c):
        out, lse = attend(d16_ref[0, r], d16_ref[1, r], d16_ref[2, r], 2, True)
        r2c_ref[0, r] = out
        r2c_ref[1, r] = lse
        return c
    for r in range(16):
        pair2_class(r, 0)

    for c in range(2):
        for r4 in range(4):
            for hi in range(4):
                r2m_ref[c, r4, pl.ds(hi, len16, stride=4), :] = r2c_ref[c, 4 * hi + r4]
        for r4 in range(4):
            r1_ref[c, pl.ds(r4, len4, stride=4), :] = r1c_ref[c, r4]
            r2_ref[c, pl.ds(r4, len4, stride=4), :] = r2m_ref[c, r4]

    chunk = 256

    def combine(i, carry):
        rs = pl.ds(pl.multiple_of(i * chunk, chunk), chunk)
        l0, l1, l2 = r0_ref[1, rs, :], r1_ref[1, rs, :], r2_ref[1, rs, :]
        mx = jnp.maximum(jnp.maximum(l0, l1), l2)
        e0, e1, e2 = jnp.exp(l0 - mx), jnp.exp(l1 - mx), jnp.exp(l2 - mx)
        num = e0 * r0_ref[0, rs, :] + e1 * r1_ref[0, rs, :] + e2 * r2_ref[0, rs, :]
        o_ref[0, rs, :] = (num / (e0 + e1 + e2)).astype(o_ref.dtype)
        return carry

    lax.fori_loop(0, seq // chunk, combine, 0)


def _attention_a(proj_a, bias_a):
    batch, seq, _ = proj_a.shape
    pairs = A_HEADS // 2
    n_pairs = len(A_PAIRS)
    col = lambda off: pl.BlockSpec((1, seq, LANES), lambda hp, b: (b, 0, off + hp))
    result = pltpu.VMEM((2, seq, LANES), F32)
    by4 = pltpu.VMEM((2, 4, seq // 4, LANES), F32)
    return pl.pallas_call(
        _attn_a_kernel,
        grid=(pairs, batch),
        in_specs=[col(0), col(pairs), col(2 * pairs),
                  pl.BlockSpec((2, n_pairs, A_BLK, 2 * A_BLK), lambda hp, b: (hp, 0, 0, 0))],
        out_specs=pl.BlockSpec((1, seq, LANES), lambda hp, b: (b, 0, hp)),
        out_shape=jax.ShapeDtypeStruct((batch, seq, A_HEADS * HEAD_DIM), BF16),
        scratch_shapes=[pltpu.VMEM((3, 4, seq // 4, LANES), F32), pltpu.VMEM((3, 16, seq // 16, LANES), F32),
                        result, by4, result, pltpu.VMEM((2, 16, seq // 16, LANES), F32), by4, result],
        compiler_params=pltpu.CompilerParams(dimension_semantics=("arbitrary", "arbitrary"),
                                             vmem_limit_bytes=VMEM_LIMIT),
        name="attention_a",
    )(proj_a, proj_a, proj_a, bias_a)


def _softmax_step(s, v_aug, m, acc):
    m_new = jnp.maximum(m, jnp.max(s, axis=-1, keepdims=True))
    alpha = jnp.exp(m - m_new)
    pr = jnp.exp(s - m_new).astype(BF16)
    return m_new, acc * alpha + jnp.dot(pr, v_aug, preferred_element_type=F32)


def _attn_b_kernel(q_ref, k_ref, v_ref, bias_ref, lam_ref, g_ref, o_ref, *, lam_init):
    qi = pl.program_id(2)
    q = q_ref[0]
    lane = lax.broadcasted_iota(jnp.int32, (BQ, LANES), 1)
    ones = jnp.ones((BK, LANES), BF16)
    scale = B_QK_DIM ** -0.5
    lf = lam_ref[...]
    dot_rows = lambda a, b: jnp.sum(lf[a:a + 1] * lf[b:b + 1], axis=-1, keepdims=True)
    lam = jnp.exp(dot_rows(0, 1)) - jnp.exp(dot_rows(2, 3)) + lam_init

    qs = [jnp.where((lane >= c * B_QK_DIM) & (lane < (c + 1) * B_QK_DIM), q, jnp.zeros_like(q))
          for c in range(4)]

    def step(ki, carry):
        off = pl.multiple_of(ki * BK, BK)
        k = k_ref[0, pl.ds(off, BK), :]
        v_aug = jnp.concatenate([v_ref[0, pl.ds(off, BK), :], ones], axis=1)
        out = []
        for c in range(4):
            bias = bias_ref[c // 2, qi - ki]
            out.append(_softmax_step(_nt_dot(qs[c], k) * scale + bias, v_aug, *carry[c]))
        return tuple(out)

    m0 = jnp.full((BQ, 1), MASK_VALUE, F32)
    a0 = jnp.zeros((BQ, 2 * LANES), F32)
    chains = lax.fori_loop(0, qi + 1, step, ((m0, a0),) * 4)

    outs = []
    for j in range(2):
        base = j * HEAD_DIM
        a1, a2 = chains[2 * j][1], chains[2 * j + 1][1]
        o = a1[:, :LANES] / a1[:, LANES:] - lam * (a2[:, :LANES] / a2[:, LANES:])
        in_head = (lane >= base) & (lane < base + HEAD_DIM)
        ms = jnp.sum(jnp.where(in_head, o * o, 0.0), axis=-1, keepdims=True) * (1.0 / HEAD_DIM)
        outs.append(o * lax.rsqrt(ms + SUBLN_EPS) * g_ref[...] * (1.0 - lam_init))
    o_ref[0] = jnp.where(lane < HEAD_DIM, outs[0], outs[1]).astype(o_ref.dtype)


def _attention_b(proj_b, bias_b, diff_lambda, subln_g2, lam_init):
    batch, seq, _ = proj_b.shape
    pairs = B_HEADS // 2
    n_tiles = seq // BQ
    return pl.pallas_call(
        functools.partial(_attn_b_kernel, lam_init=lam_init),
        grid=(pairs, batch, n_tiles),
        in_specs=[pl.BlockSpec((1, BQ, LANES), lambda hp, b, i: (b, i, hp)),
                  pl.BlockSpec((1, seq, LANES), lambda hp, b, i: (b, 0, pairs + hp)),
                  pl.BlockSpec((1, seq, LANES), lambda hp, b, i: (b, 0, 2 * pairs + hp)),
                  pl.BlockSpec((2, n_tiles, BQ, BK), lambda hp, b, i: (hp, 0, 0, 0),
                               pipeline_mode=pl.Buffered(1)),
                  pl.BlockSpec(diff_lambda.shape, lambda hp, b, i: (0, 0)),
                  pl.BlockSpec(subln_g2.shape, lambda hp, b, i: (0, 0))],
        out_specs=pl.BlockSpec((1, BQ, LANES), lambda hp, b, i: (b, i, hp)),
        out_shape=jax.ShapeDtypeStruct((batch, seq, B_HEADS * HEAD_DIM), BF16),
        compiler_params=pltpu.CompilerParams(dimension_semantics=("arbitrary",) * 3,
                                             vmem_limit_bytes=VMEM_LIMIT),
        name="attention_b",
    )(proj_b, proj_b, proj_b, bias_b, diff_lambda, subln_g2)


def _attn_c_kernel(q_ref, k_ref, v_ref, o_ref):
    qi = pl.program_id(2)
    lane = lax.broadcasted_iota(jnp.int32, (BQ, LANES), 1)
    ones = jnp.ones((BK, LANES), BF16)
    scale = (C_NOPE + C_ROPE) ** -0.5
    row = lax.broadcasted_iota(jnp.int32, (BQ, BK), 0)
    colm = lax.broadcasted_iota(jnp.int32, (BQ, BK), 1)
    diag_bias = jnp.where(row >= colm, 0.0, MASK_VALUE)

    def step(ki, carry, masked):
        off = pl.multiple_of(ki * BK, BK)
        v_aug = jnp.concatenate([v_ref[0, pl.ds(off, BK), :], ones], axis=1)
        out = []
        for j in range(2):
            hs = slice(j * LANES, (j + 1) * LANES)
            s = _nt_dot(q_ref[0, :, hs], k_ref[0, pl.ds(off, BK), hs]) * scale
            if masked:
                s = s + diag_bias
            out.append(_softmax_step(s, v_aug, *carry[j]))
        return tuple(out)

    m0 = jnp.full((BQ, 1), MASK_VALUE, F32)
    a0 = jnp.zeros((BQ, 2 * LANES), F32)
    carry = lax.fori_loop(0, qi, functools.partial(step, masked=False), ((m0, a0),) * 2)
    (_, acc0), (_, acc1) = step(qi, carry, True)
    o_ref[0] = jnp.where(lane < HEAD_DIM, acc0[:, :LANES] / acc0[:, LANES:],
                         acc1[:, :LANES] / acc1[:, LANES:]).astype(o_ref.dtype)


def _attention_c(qc, kc, vc):
    batch, seq, _ = qc.shape
    pairs = C_HEADS // 2
    n_tiles = seq // BQ
    return pl.pallas_call(
        _attn_c_kernel,
        grid=(pairs, batch, n_tiles),
        in_specs=[pl.BlockSpec((1, BQ, 2 * LANES), lambda hp, b, i: (b, i, hp)),
                  pl.BlockSpec((1, seq, 2 * LANES), lambda hp, b, i: (b, 0, hp)),
                  pl.BlockSpec((1, seq, LANES), lambda hp, b, i: (b, 0, hp))],
        out_specs=pl.BlockSpec((1, BQ, LANES), lambda hp, b, i: (b, i, hp)),
        out_shape=jax.ShapeDtypeStruct((batch, seq, C_HEADS * HEAD_DIM), BF16),
        compiler_params=pltpu.CompilerParams(dimension_semantics=("arbitrary",) * 3,
                                             vmem_limit_bytes=VMEM_LIMIT),
        name="attention_c",
    )(qc, kc, vc)


def _layer_norm(y, g, b):
    mu = jnp.mean(y, axis=-1, keepdims=True)
    yc = y - mu
    var = jnp.mean(yc * yc, axis=-1, keepdims=True)
    return yc * lax.rsqrt(var + LN_EPS) * g + b


def _out_ffn_kernel(x_ref, oa_ref, ob_ref, oc_ref, wo_ref, g1_ref, b1_ref, g2_ref, b2_ref,
                    wg_ref, wu_ref, wd_ref, out_ref, *, alpha):
    heads = jnp.concatenate([oa_ref[...], ob_ref[...], oc_ref[...]], axis=1)
    mix = jnp.dot(heads, wo_ref[...], preferred_element_type=F32)
    h = _layer_norm(alpha * x_ref[...] + mix, g1_ref[...], b1_ref[...])
    hb = h.astype(BF16)
    ffn = jnp.zeros(h.shape, F32)
    for c in range(FF_DIM // FFN_CHUNK):
        cs = slice(c * FFN_CHUNK, (c + 1) * FFN_CHUNK)
        gate = jnp.dot(hb, wg_ref[:, cs], preferred_element_type=F32)
        up = jnp.dot(hb, wu_ref[:, cs], preferred_element_type=F32)
        act = (jax.nn.silu(gate) * up).astype(BF16)
        ffn = ffn + jnp.dot(act, wd_ref[cs, :], preferred_element_type=F32)
    out_ref[...] = _layer_norm(alpha * h + ffn, g2_ref[...], b2_ref[...])


def _out_ffn(x2d, oa, ob, oc, wo, g1, b1, g2, b2, wg, wu, wd, alpha):
    n_tok = x2d.shape[0]
    tm = FFN_TM
    tok = lambda width: pl.BlockSpec((tm, width), lambda i: (i, 0))
    once = lambda a: pl.BlockSpec(a.shape, lambda i: (0, 0), pipeline_mode=pl.Buffered(1))
    return pl.pallas_call(
        functools.partial(_out_ffn_kernel, alpha=alpha),
        grid=(n_tok // tm,),
        in_specs=[tok(D_MODEL), tok(oa.shape[1]), tok(ob.shape[1]), tok(oc.shape[1]),
                  once(wo), once(g1), once(b1), once(g2), once(b2), once(wg), once(wu), once(wd)],
        out_specs=tok(D_MODEL),
        out_shape=jax.ShapeDtypeStruct((n_tok, D_MODEL), F32),
        compiler_params=pltpu.CompilerParams(dimension_semantics=("arbitrary",),
                                             vmem_limit_bytes=VMEM_LIMIT),
        name="out_ffn",
    )(x2d, oa, ob, oc, wo, g1, b1, g2, b2, wg, wu, wd)


def _rotate_half_cols(w):
    half = w.shape[-1] // 2
    return jnp.concatenate([-w[..., half:], w[..., :half]], axis=-1)


def _place_rope(w):
    return jnp.pad(w, ((0, 0), (C_NOPE, LANES - C_NOPE - C_ROPE)))


def _layer_weights(w_in, w_uq, w_ukv):
    main = A_COLS + B_COLS + C_Q_LORA + C_KV_LORA
    k_r = w_in[:, main:]
    w1 = jnp.concatenate([w_in[:, :main], _place_rope(k_r), _place_rope(_rotate_half_cols(k_r))],
                         axis=1).astype(BF16)
    rows = w_uq.shape[0]
    uq = w_uq.reshape(rows, C_HEADS, C_NOPE + C_ROPE)
    nope, rope = uq[..., :C_NOPE], uq[..., C_NOPE:]
    tail = jnp.zeros((rows, C_HEADS, LANES - C_NOPE - C_ROPE), w_uq.dtype)
    q_main = jnp.concatenate([nope, rope, tail], axis=-1).reshape(rows, C_HEADS * LANES)
    q_half = jnp.concatenate([jnp.zeros_like(nope), _rotate_half_cols(rope), tail],
                             axis=-1).reshape(rows, C_HEADS * LANES)
    wq = jnp.concatenate([q_main, q_half], axis=1).astype(BF16)
    rows = w_ukv.shape[0]
    ukv = w_ukv.reshape(rows, C_HEADS, C_NOPE + HEAD_DIM)
    k_nope, v = ukv[..., :C_NOPE], ukv[..., C_NOPE:]
    k_main = jnp.concatenate([k_nope, jnp.zeros((rows, C_HEADS, LANES - C_NOPE), w_ukv.dtype)],
                             axis=-1).reshape(rows, C_HEADS * LANES)
    wkv = jnp.concatenate([k_main, v.reshape(rows, C_HEADS * HEAD_DIM)], axis=1).astype(BF16)
    return w1, wq, wkv


def _rope_tables(seq):
    half = C_ROPE // 2
    inv = ROPE_THETA ** (-np.arange(half, dtype=np.float64) / half)
    ang = np.arange(seq, dtype=np.float64)[:, None] * inv[None, :]
    cos = np.zeros((seq, LANES), np.float64)
    sin = np.zeros((seq, LANES), np.float64)
    cos[:, :C_NOPE] = 1.0
    cos[:, C_NOPE:C_NOPE + C_ROPE] = np.concatenate([np.cos(ang), np.cos(ang)], axis=1)
    sin[:, C_NOPE:C_NOPE + C_ROPE] = np.concatenate([np.sin(ang), np.sin(ang)], axis=1)
    return jnp.asarray(cos, F32), jnp.asarray(sin, F32)


def kernel(x, rel_bias, w_in, q_norm_g, kv_norm_g, w_uq, w_ukv, diff_lambda, subln_g, w_o, ln1_g, ln1_b, ln2_g, ln2_b, w_gate, w_up, w_down):
    batch, seq, d_model = x.shape
    depth = w_in.shape[0]
    alpha = (2 * depth) ** 0.25
    bias_a = _bias_tiles(rel_bias, A_HEADS, 0, len(A_PAIRS), A_BLK, 2 * A_BLK, True)
    bias_b = _bias_tiles(rel_bias, B_HEADS, A_HEADS, seq // BQ, BQ, BK, False)
    cos_t, sin_t = _rope_tables(seq)
    row = lambda a: a.reshape(1, -1)
    x2d = x.reshape(batch * seq, d_model)
    for l in range(depth):
        w1, wq, wkv = _layer_weights(w_in[l], w_uq[l], w_ukv[l])
        pa, pb, qc, kc, vc = _projection(x2d, w1, row(q_norm_g[l]), row(kv_norm_g[l]), wq, wkv,
                                         cos_t, sin_t, seq)
        lam_init = 0.8 - 0.6 * math.exp(-0.3 * l)
        oa = _attention_a(pa.reshape(batch, seq, -1), bias_a)
        ob = _attention_b(pb.reshape(batch, seq, -1), bias_b, diff_lambda[l],
                          row(jnp.concatenate([subln_g[l], subln_g[l]])), lam_init)
        oc = _attention_c(qc.reshape(batch, seq, -1), kc.reshape(batch, seq, -1),
                          vc.reshape(batch, seq, -1))
        flat = lambda a: a.reshape(batch * seq, -1)
        x2d = _out_ffn(x2d, flat(oa), flat(ob), flat(oc), w_o[l].astype(BF16),
                       row(ln1_g[l]), row(ln1_b[l]), row(ln2_g[l]), row(ln2_b[l]),
                       w_gate[l].astype(BF16), w_up[l].astype(BF16), w_down[l].astype(BF16), alpha)
    return x2d.reshape(batch, seq, d_model)
```

```python
import functools
import math

import numpy as np
import jax
import jax.numpy as jnp
from jax import lax
from jax.experimental import pallas as pl
from jax.experimental.pallas import tpu as pltpu

F32 = jnp.float32
BF16 = jnp.bfloat16

D_MODEL = 1024
HEAD_DIM = 64
A_HEADS = 6
A_PAIRS = ((128, 1), (512, 4), (2048, 16))
A_BLK = 128
A_PAIR0_UNROLL = 5
B_HEADS = 4
B_QK_DIM = 32
C_HEADS = 6
C_Q_LORA = 256
C_KV_LORA = 128
C_NOPE = 64
C_ROPE = 32
ROPE_THETA = 10000.0
A_COLS = 3 * A_HEADS * HEAD_DIM
B_COLS = B_HEADS * 3 * HEAD_DIM
NUM_BUCKETS = 32
MAX_DISTANCE = 2048
FF_DIM = 2816
LN_EPS = 1e-5
LATENT_EPS = 1e-6
SUBLN_EPS = 1e-5

LANES = 128
MASK_VALUE = -1e30
LOG2E = math.log2(math.e)
A_QSCALE = HEAD_DIM ** -0.5 * LOG2E
B_QSCALE = B_QK_DIM ** -0.5 * LOG2E
C_QSCALE = (C_NOPE + C_ROPE) ** -0.5 * LOG2E
VMEM_LIMIT = 56 * 1024 * 1024

PROJ_TM = 512
FFN_TM = 512
FFN_CHUNK = 1408
BQ = 512
BK = 512
QK_RATIO = BQ // BK


def _bucket_lower_bounds():
    max_exact = NUM_BUCKETS // 2
    d = np.arange(0, MAX_DISTANCE + 1)
    val = np.log(np.maximum(d, 1) / max_exact) / math.log(MAX_DISTANCE / max_exact) * (NUM_BUCKETS - max_exact)
    large = np.minimum(max_exact + np.floor(np.maximum(val, 0.0)).astype(np.int64), NUM_BUCKETS - 1)
    bucket = np.where(d < max_exact, d, large)
    return [int(np.argmax(bucket >= b)) for b in range(NUM_BUCKETS)]


_BUCKET_LO = _bucket_lower_bounds()


def _nt_dot(a, b):
    return lax.dot_general(a, b, (((1,), (1,)), ((), ())), preferred_element_type=F32)


def _own_head_lanes(rows):
    lane = lax.broadcasted_iota(jnp.int32, (rows, LANES), 1)
    return lane < HEAD_DIM, lane >= HEAD_DIM


def _values_and_ones(v, own):
    return jnp.where(own, v, jnp.ones_like(v))


def _divide_by_row_sum(acc):
    return acc / pltpu.roll(acc, HEAD_DIM, 1)


def _bias_tile_kernel(rb_ref, out_ref, *, head_offset, dilated, g_shift):
    h = pl.program_id(0) + head_offset
    g = pl.program_id(1)
    rows, cols = out_ref.shape[2], out_ref.shape[3]
    i = lax.broadcasted_iota(jnp.int32, (rows, cols), 0)
    j = lax.broadcasted_iota(jnp.int32, (rows, cols), 1)
    if dilated:
        rel = i + A_BLK - j
        valid = (rel >= 0) & (rel <= A_BLK)
        dil = jnp.where(g == 0, A_PAIRS[0][1], jnp.where(g == 1, A_PAIRS[1][1], A_PAIRS[2][1]))
        dist = rel * dil
    else:
        dist = (g - g_shift) * cols + i - j
        valid = dist >= 0
    val = jnp.full((rows, cols), rb_ref[0, h], F32)
    for b in range(1, NUM_BUCKETS):
        val = jnp.where(dist >= _BUCKET_LO[b], rb_ref[b, h], val)
    out_ref[0, 0] = jnp.where(valid, val * LOG2E, MASK_VALUE)


def _bias_tiles(rel_bias, n_heads, head_offset, n_groups, rows, cols, dilated, g_shift=0):
    return pl.pallas_call(
        functools.partial(_bias_tile_kernel, head_offset=head_offset, dilated=dilated, g_shift=g_shift),
        grid=(n_heads, n_groups),
        in_specs=[pl.BlockSpec(memory_space=pltpu.SMEM)],
        out_specs=pl.BlockSpec((1, 1, rows, cols), lambda h, g: (h, g, 0, 0)),
        out_shape=jax.ShapeDtypeStruct((n_heads, n_groups, rows, cols), F32),
        name="bias_tiles",
    )(rel_bias)


def _proj_kernel(x_ref, w1_ref, qg_ref, kvg_ref, wq_ref, wkv_ref, cos_ref, sin_ref,
                 pa_ref, pb_ref, qc_ref, kc_ref, vc_ref):
    xb = x_ref[...].astype(BF16)
    a_q = A_HEADS * HEAD_DIM
    pa_ref[:, :a_q] = jnp.dot(xb, w1_ref[:, :a_q], preferred_element_type=F32) * A_QSCALE
    pa_ref[:, a_q:] = jnp.dot(xb, w1_ref[:, a_q:A_COLS], preferred_element_type=F32)
    b_q = B_HEADS * HEAD_DIM
    pb_ref[:, :b_q] = (jnp.dot(xb, w1_ref[:, A_COLS:A_COLS + b_q], preferred_element_type=F32)
                       * B_QSCALE).astype(BF16)
    pb_ref[:, b_q:] = jnp.dot(xb, w1_ref[:, A_COLS + b_q:A_COLS + B_COLS],
                              preferred_element_type=F32).astype(BF16)
    c = jnp.dot(xb, w1_ref[:, A_COLS + B_COLS:], preferred_element_type=F32)
    c_q = c[:, :C_Q_LORA]
    c_kv = c[:, C_Q_LORA:C_Q_LORA + C_KV_LORA]
    k_rot = c[:, C_Q_LORA + C_KV_LORA:C_Q_LORA + C_KV_LORA + LANES]
    k_rot_half = c[:, C_Q_LORA + C_KV_LORA + LANES:]
    cqn = (c_q * lax.rsqrt(jnp.mean(c_q * c_q, axis=-1, keepdims=True) + LATENT_EPS) * qg_ref[...]).astype(BF16)
    ckvn = (c_kv * lax.rsqrt(jnp.mean(c_kv * c_kv, axis=-1, keepdims=True) + LATENT_EPS) * kvg_ref[...]).astype(BF16)
    q2 = jnp.dot(cqn, wq_ref[...], preferred_element_type=F32)
    kv = jnp.dot(ckvn, wkv_ref[...], preferred_element_type=F32)
    cos = cos_ref[...]
    sin = sin_ref[...]
    k_rope = k_rot * cos + k_rot_half * sin
    qw = C_HEADS * LANES
    for h in range(C_HEADS):
        sl = slice(h * LANES, (h + 1) * LANES)
        sl2 = slice(qw + h * LANES, qw + (h + 1) * LANES)
        qc_ref[:, sl] = ((q2[:, sl] * cos + q2[:, sl2] * sin) * C_QSCALE).astype(BF16)
        kc_ref[:, sl] = (kv[:, sl] + k_rope).astype(BF16)
    vc_ref[...] = kv[:, qw:].astype(BF16)


def _projection(x2d, w1, qg, kvg, wq, wkv, cos_t, sin_t, seq):
    n_tok = x2d.shape[0]
    tm = PROJ_TM
    pos_blocks = seq // tm
    full = lambda shape: pl.BlockSpec(shape, lambda i: (0, 0))
    tok = lambda width: pl.BlockSpec((tm, width), lambda i: (i, 0))
    pos = pl.BlockSpec((tm, LANES), lambda i: (i % pos_blocks, 0))
    return pl.pallas_call(
        _proj_kernel,
        grid=(n_tok // tm,),
        in_specs=[tok(D_MODEL), full(w1.shape), full(qg.shape), full(kvg.shape), full(wq.shape),
                  full(wkv.shape), pos, pos],
        out_specs=[tok(A_COLS), tok(B_COLS), tok(C_HEADS * LANES), tok(C_HEADS * LANES),
                   tok(C_HEADS * HEAD_DIM)],
        out_shape=[jax.ShapeDtypeStruct((n_tok, A_COLS), F32),
                   jax.ShapeDtypeStruct((n_tok, B_COLS), BF16),
                   jax.ShapeDtypeStruct((n_tok, C_HEADS * LANES), BF16),
                   jax.ShapeDtypeStruct((n_tok, C_HEADS * LANES), BF16),
                   jax.ShapeDtypeStruct((n_tok, C_HEADS * HEAD_DIM), BF16)],
        compiler_params=pltpu.CompilerParams(dimension_semantics=("arbitrary",),
                                             vmem_limit_bytes=VMEM_LIMIT),
        name="projection",
    )(x2d, w1, qg, kvg, wq, wkv, cos_t, sin_t)


def _attn_a_kernel(q_ref, k_ref, v_ref, bias_ref, o_ref, d4_ref, d16_ref, r0_ref, r1c_ref, r1_ref,
                   r2c_ref, r2m_ref, r2_ref):
    seq = q_ref.shape[1]
    len4, len16 = seq // 4, seq // 16
    lane = lax.broadcasted_iota(jnp.int32, (A_BLK, LANES), 1)
    lo_half = lane < HEAD_DIM
    ones = jnp.ones((2 * A_BLK, LANES), BF16)

    for a, src in enumerate((q_ref, k_ref, v_ref)):
        for r4 in range(4):
            d4_ref[a, r4] = src[0, pl.ds(r4, len4, stride=4), :]
        for r4 in range(4):
            for hi in range(4):
                d16_ref[a, 4 * hi + r4] = d4_ref[a, r4, pl.ds(hi, len16, stride=4), :]

    def attend(qf, kf, vf, p, first):
        nk = kf.shape[0]
        kb = kf.astype(BF16)
        v_aug = jnp.concatenate([vf.astype(BF16), ones[:nk]], axis=1)
        out, lse = [], []
        for j in range(2):
            qj = jnp.where(lo_half if j == 0 else jnp.logical_not(lo_half), qf, 0.0).astype(BF16)
            bias = bias_ref[j, p]
            if first:
                bias = bias[:, A_BLK:]
            s = _nt_dot(qj, kb) + bias
            m = jnp.max(s, axis=-1, keepdims=True)
            pr = jnp.exp2(s - m).astype(BF16)
            r = jnp.dot(pr, v_aug, preferred_element_type=F32)
            den = r[:, LANES:]
            out.append(r[:, :LANES] / den)
            lse.append(m + jnp.log2(den))
        return jnp.where(lo_half, out[0], out[1]), jnp.where(lo_half, lse[0], lse[1])

    def key_rows(start, first):
        return pl.ds(start, A_BLK) if first else pl.ds(start - A_BLK, 2 * A_BLK)

    def pair0_block(start, first):
        kr = key_rows(start, first)
        out, lse = attend(q_ref[0, pl.ds(start, A_BLK), :], k_ref[0, kr, :], v_ref[0, kr, :], 0, first)
        r0_ref[0, pl.ds(start, A_BLK), :] = out
        r0_ref[1, pl.ds(start, A_BLK), :] = lse

    pair0_block(0, True)

    def pair0_later(n, c):
        pair0_block(pl.multiple_of(n * A_BLK, A_BLK), False)
        return c
    lax.fori_loop(1, seq // A_BLK, pair0_later, 0, unroll=A_PAIR0_UNROLL)

    def pair1_class(r4, c):
        for n in range(len4 // A_BLK):
            kr = key_rows(n * A_BLK, n == 0)
            out, lse = attend(d4_ref[0, r4, pl.ds(n * A_BLK, A_BLK), :], d4_ref[1, r4, kr, :],
                              d4_ref[2, r4, kr, :], 1, n == 0)
            r1c_ref[0, r4, pl.ds(n * A_BLK, A_BLK), :] = out
            r1c_ref[1, r4, pl.ds(n * A_BLK, A_BLK), :] = lse
        return c
    for r4 in range(4):
        pair1_class(r4, 0)

    def pair2_class(r, c):
        out, lse = attend(d16_ref[0, r], d16_ref[1, r], d16_ref[2, r], 2, True)
        r2c_ref[0, r] = out
        r2c_ref[1, r] = lse
        return c
    for r in range(16):
        pair2_class(r, 0)

    for c in range(2):
        for r4 in range(4):
            for hi in range(4):
                r2m_ref[c, r4, pl.ds(hi, len16, stride=4), :] = r2c_ref[c, 4 * hi + r4]
        for r4 in range(4):
            r1_ref[c, pl.ds(r4, len4, stride=4), :] = r1c_ref[c, r4]
            r2_ref[c, pl.ds(r4, len4, stride=4), :] = r2m_ref[c, r4]

    chunk = 256

    def combine(i, carry):
        rs = pl.ds(pl.multiple_of(i * chunk, chunk), chunk)
        l0, l1, l2 = r0_ref[1, rs, :], r1_ref[1, rs, :], r2_ref[1, rs, :]
        mx = jnp.maximum(jnp.maximum(l0, l1), l2)
        e0, e1, e2 = jnp.exp2(l0 - mx), jnp.exp2(l1 - mx), jnp.exp2(l2 - mx)
        num = e0 * r0_ref[0, rs, :] + e1 * r1_ref[0, rs, :] + e2 * r2_ref[0, rs, :]
        o_ref[0, rs, :] = (num / (e0 + e1 + e2)).astype(o_ref.dtype)
        return carry

    lax.fori_loop(0, seq // chunk, combine, 0)


def _attention_a(proj_a, bias_a):
    batch, seq, _ = proj_a.shape
    pairs = A_HEADS // 2
    n_pairs = len(A_PAIRS)
    col = lambda off: pl.BlockSpec((1, seq, LANES), lambda hp, b: (b, 0, off + hp))
    result = pltpu.VMEM((2, seq, LANES), F32)
    by4 = pltpu.VMEM((2, 4, seq // 4, LANES), F32)
    return pl.pallas_call(
        _attn_a_kernel,
        grid=(pairs, batch),
        in_specs=[col(0), col(pairs), col(2 * pairs),
                  pl.BlockSpec((2, n_pairs, A_BLK, 2 * A_BLK), lambda hp, b: (hp, 0, 0, 0))],
        out_specs=pl.BlockSpec((1, seq, LANES), lambda hp, b: (b, 0, hp)),
        out_shape=jax.ShapeDtypeStruct((batch, seq, A_HEADS * HEAD_DIM), BF16),
        scratch_shapes=[pltpu.VMEM((3, 4, seq // 4, LANES), F32), pltpu.VMEM((3, 16, seq // 16, LANES), F32),
                        result, by4, result, pltpu.VMEM((2, 16, seq // 16, LANES), F32), by4, result],
        compiler_params=pltpu.CompilerParams(dimension_semantics=("arbitrary", "arbitrary"),
                                             vmem_limit_bytes=VMEM_LIMIT),
        name="attention_a",
    )(proj_a, proj_a, proj_a, bias_a)


def _softmax_step(s, v_aug, m, acc):
    m_new = jnp.maximum(m, jnp.max(s, axis=-1, keepdims=True))
    alpha = jnp.exp2(m - m_new)
    pr = jnp.exp2(s - m_new).astype(BF16)
    return m_new, acc * alpha + jnp.dot(pr, v_aug, preferred_element_type=F32)


def _attn_b_kernel(q_ref, k_ref, v_ref, bias_ref, lam_ref, g_ref, o_ref, *, lam_init):
    qi = pl.program_id(2)
    q = q_ref[0]
    lane = lax.broadcasted_iota(jnp.int32, (BQ, LANES), 1)
    own = _own_head_lanes(BK)
    lf = lam_ref[...]
    dot_rows = lambda a, b: jnp.sum(lf[a:a + 1] * lf[b:b + 1], axis=-1, keepdims=True)
    lam = jnp.exp(dot_rows(0, 1)) - jnp.exp(dot_rows(2, 3)) + lam_init

    qs = [jnp.where((lane >= c * B_QK_DIM) & (lane < (c + 1) * B_QK_DIM), q, jnp.zeros_like(q))
          for c in range(4)]

    def step(ki, carry):
        off = pl.multiple_of(ki * BK, BK)
        k = k_ref[0, pl.ds(off, BK), :]
        v = v_ref[0, pl.ds(off, BK), :]
        v_aug = [_values_and_ones(v, own[j]) for j in range(2)]
        out = []
        for c in range(4):
            bias = bias_ref[c // 2, QK_RATIO * qi + (QK_RATIO - 1) - ki]
            out.append(_softmax_step(_nt_dot(qs[c], k) + bias, v_aug[c // 2], *carry[c]))
        return tuple(out)

    m0 = jnp.full((BQ, 1), MASK_VALUE, F32)
    a0 = jnp.zeros((BQ, LANES), F32)
    chains = lax.fori_loop(0, QK_RATIO * (qi + 1), step, ((m0, a0),) * 4)

    outs = []
    for j in range(2):
        base = j * HEAD_DIM
        a1, a2 = chains[2 * j][1], chains[2 * j + 1][1]
        o = _divide_by_row_sum(a1) - lam * _divide_by_row_sum(a2)
        in_head = (lane >= base) & (lane < base + HEAD_DIM)
        ms = jnp.sum(jnp.where(in_head, o * o, 0.0), axis=-1, keepdims=True) * (1.0 / HEAD_DIM)
        outs.append(o * lax.rsqrt(ms + SUBLN_EPS) * g_ref[...] * (1.0 - lam_init))
    o_ref[0] = jnp.where(lane < HEAD_DIM, outs[0], outs[1]).astype(o_ref.dtype)


def _attention_b(proj_b, bias_b, diff_lambda, subln_g2, lam_init):
    batch, seq, _ = proj_b.shape
    pairs = B_HEADS // 2
    n_tiles = seq // BQ
    return pl.pallas_call(
        functools.partial(_attn_b_kernel, lam_init=lam_init),
        grid=(pairs, batch, n_tiles),
        in_specs=[pl.BlockSpec((1, BQ, LANES), lambda hp, b, i: (b, i, hp)),
                  pl.BlockSpec((1, seq, LANES), lambda hp, b, i: (b, 0, pairs + hp)),
                  pl.BlockSpec((1, seq, LANES), lambda hp, b, i: (b, 0, 2 * pairs + hp)),
                  pl.BlockSpec((2, seq // BK, BQ, BK), lambda hp, b, i: (hp, 0, 0, 0),
                               pipeline_mode=pl.Buffered(1)),
                  pl.BlockSpec(diff_lambda.shape, lambda hp, b, i: (0, 0)),
                  pl.BlockSpec(subln_g2.shape, lambda hp, b, i: (0, 0))],
        out_specs=pl.BlockSpec((1, BQ, LANES), lambda hp, b, i: (b, i, hp)),
        out_shape=jax.ShapeDtypeStruct((batch, seq, B_HEADS * HEAD_DIM), BF16),
        compiler_params=pltpu.CompilerParams(dimension_semantics=("arbitrary",) * 3,
                                             vmem_limit_bytes=VMEM_LIMIT),
        name="attention_b",
    )(proj_b, proj_b, proj_b, bias_b, diff_lambda, subln_g2)


def _attn_c_kernel(q_ref, k_ref, v_ref, o_ref):
    qi = pl.program_id(2)
    lane = lax.broadcasted_iota(jnp.int32, (BQ, LANES), 1)
    own = _own_head_lanes(BK)
    row = lax.broadcasted_iota(jnp.int32, (BQ, BK), 0)
    colm = lax.broadcasted_iota(jnp.int32, (BQ, BK), 1)

    def step(ki, carry, masked):
        off = pl.multiple_of(ki * BK, BK)
        v = v_ref[0, pl.ds(off, BK), :]
        out = []
        for j in range(2):
            hs = slice(j * LANES, (j + 1) * LANES)
            s = _nt_dot(q_ref[0, :, hs], k_ref[0, pl.ds(off, BK), hs])
            if masked is not None:
                s = s + jnp.where(row - colm >= masked * BK, 0.0, MASK_VALUE)
            out.append(_softmax_step(s, _values_and_ones(v, own[j]), *carry[j]))
        return tuple(out)

    m0 = jnp.full((BQ, 1), MASK_VALUE, F32)
    a0 = jnp.zeros((BQ, LANES), F32)
    carry = lax.fori_loop(0, QK_RATIO * qi, functools.partial(step, masked=None), ((m0, a0),) * 2)
    for t in range(QK_RATIO):
        carry = step(QK_RATIO * qi + t, carry, t)
    (_, acc0), (_, acc1) = carry
    o_ref[0] = jnp.where(lane < HEAD_DIM, _divide_by_row_sum(acc0),
                         _divide_by_row_sum(acc1)).astype(o_ref.dtype)


def _attention_c(qc, kc, vc):
    batch, seq, _ = qc.shape
    pairs = C_HEADS // 2
    n_tiles = seq // BQ
    return pl.pallas_call(
        _attn_c_kernel,
        grid=(pairs, batch, n_tiles),
        in_specs=[pl.BlockSpec((1, BQ, 2 * LANES), lambda hp, b, i: (b, i, hp)),
                  pl.BlockSpec((1, seq, 2 * LANES), lambda hp, b, i: (b, 0, hp)),
                  pl.BlockSpec((1, seq, LANES), lambda hp, b, i: (b, 0, hp))],
        out_specs=pl.BlockSpec((1, BQ, LANES), lambda hp, b, i: (b, i, hp)),
        out_shape=jax.ShapeDtypeStruct((batch, seq, C_HEADS * HEAD_DIM), BF16),
        compiler_params=pltpu.CompilerParams(dimension_semantics=("arbitrary",) * 3,
                                             vmem_limit_bytes=VMEM_LIMIT),
        name="attention_c",
    )(qc, kc, vc)


def _layer_norm(y, g, b):
    mu = jnp.mean(y, axis=-1, keepdims=True)
    yc = y - mu
    var = jnp.mean(yc * yc, axis=-1, keepdims=True)
    return yc * lax.rsqrt(var + LN_EPS) * g + b


def _out_ffn_kernel(x_ref, oa_ref, ob_ref, oc_ref, wo_ref, g1_ref, b1_ref, g2_ref, b2_ref,
                    wg_ref, wu_ref, wd_ref, out_ref, *, alpha):
    heads = jnp.concatenate([oa_ref[...], ob_ref[...], oc_ref[...]], axis=1)
    mix = jnp.dot(heads, wo_ref[...], preferred_element_type=F32)
    h = _layer_norm(alpha * x_ref[...] + mix, g1_ref[...], b1_ref[...])
    hb = h.astype(BF16)
    ffn = jnp.zeros(h.shape, F32)
    for c in range(FF_DIM // FFN_CHUNK):
        cs = slice(c * FFN_CHUNK, (c + 1) * FFN_CHUNK)
        gate = jnp.dot(hb, wg_ref[:, cs], preferred_element_type=F32)
        up = jnp.dot(hb, wu_ref[:, cs], preferred_element_type=F32)
        act = (jax.nn.silu(gate) * up).astype(BF16)
        ffn = ffn + jnp.dot(act, wd_ref[cs, :], preferred_element_type=F32)
    out_ref[...] = _layer_norm(alpha * h + ffn, g2_ref[...], b2_ref[...])


def _out_ffn(x2d, oa, ob, oc, wo, g1, b1, g2, b2, wg, wu, wd, alpha):
    n_tok = x2d.shape[0]
    tm = FFN_TM
    tok = lambda width: pl.BlockSpec((tm, width), lambda i: (i, 0))
    once = lambda a: pl.BlockSpec(a.shape, lambda i: (0, 0), pipeline_mode=pl.Buffered(1))
    return pl.pallas_call(
        functools.partial(_out_ffn_kernel, alpha=alpha),
        grid=(n_tok // tm,),
        in_specs=[tok(D_MODEL), tok(oa.shape[1]), tok(ob.shape[1]), tok(oc.shape[1]),
                  once(wo), once(g1), once(b1), once(g2), once(b2), once(wg), once(wu), once(wd)],
        out_specs=tok(D_MODEL),
        out_shape=jax.ShapeDtypeStruct((n_tok, D_MODEL), F32),
        compiler_params=pltpu.CompilerParams(dimension_semantics=("arbitrary",),
                                             vmem_limit_bytes=VMEM_LIMIT),
        name="out_ffn",
    )(x2d, oa, ob, oc, wo, g1, b1, g2, b2, wg, wu, wd)


def _rotate_half_cols(w):
    half = w.shape[-1] // 2
    return jnp.concatenate([-w[..., half:], w[..., :half]], axis=-1)


def _place_rope(w):
    return jnp.pad(w, ((0, 0), (C_NOPE, LANES - C_NOPE - C_ROPE)))


def _layer_weights(w_in, w_uq, w_ukv):
    main = A_COLS + B_COLS + C_Q_LORA + C_KV_LORA
    k_r = w_in[:, main:]
    w1 = jnp.concatenate([w_in[:, :main], _place_rope(k_r), _place_rope(_rotate_half_cols(k_r))],
                         axis=1).astype(BF16)
    rows = w_uq.shape[0]
    uq = w_uq.reshape(rows, C_HEADS, C_NOPE + C_ROPE)
    nope, rope = uq[..., :C_NOPE], uq[..., C_NOPE:]
    tail = jnp.zeros((rows, C_HEADS, LANES - C_NOPE - C_ROPE), w_uq.dtype)
    q_main = jnp.concatenate([nope, rope, tail], axis=-1).reshape(rows, C_HEADS * LANES)
    q_half = jnp.concatenate([jnp.zeros_like(nope), _rotate_half_cols(rope), tail],
                             axis=-1).reshape(rows, C_HEADS * LANES)
    wq = jnp.concatenate([q_main, q_half], axis=1).astype(BF16)
    rows = w_ukv.shape[0]
    ukv = w_ukv.reshape(rows, C_HEADS, C_NOPE + HEAD_DIM)
    k_nope, v = ukv[..., :C_NOPE], ukv[..., C_NOPE:]
    k_main = jnp.concatenate([k_nope, jnp.zeros((rows, C_HEADS, LANES - C_NOPE), w_ukv.dtype)],
                             axis=-1).reshape(rows, C_HEADS * LANES)
    wkv = jnp.concatenate([k_main, v.reshape(rows, C_HEADS * HEAD_DIM)], axis=1).astype(BF16)
    return w1, wq, wkv


def _rope_tables(seq):
    half = C_ROPE // 2
    inv = ROPE_THETA ** (-np.arange(half, dtype=np.float64) / half)
    ang = np.arange(seq, dtype=np.float64)[:, None] * inv[None, :]
    cos = np.zeros((seq, LANES), np.float64)
    sin = np.zeros((seq, LANES), np.float64)
    cos[:, :C_NOPE] = 1.0
    cos[:, C_NOPE:C_NOPE + C_ROPE] = np.concatenate([np.cos(ang), np.cos(ang)], axis=1)
    sin[:, C_NOPE:C_NOPE + C_ROPE] = np.concatenate([np.sin(ang), np.sin(ang)], axis=1)
    return jnp.asarray(cos, F32), jnp.asarray(sin, F32)


def kernel(x, rel_bias, w_in, q_norm_g, kv_norm_g, w_uq, w_ukv, diff_lambda, subln_g, w_o, ln1_g, ln1_b, ln2_g, ln2_b, w_gate, w_up, w_down):
    batch, seq, d_model = x.shape
    depth = w_in.shape[0]
    alpha = (2 * depth) ** 0.25
    bias_a = _bias_tiles(rel_bias, A_HEADS, 0, len(A_PAIRS), A_BLK, 2 * A_BLK, True)
    bias_b = _bias_tiles(rel_bias, B_HEADS, A_HEADS, seq // BK, BQ, BK, False, QK_RATIO - 1)
    cos_t, sin_t = _rope_tables(seq)
    row = lambda a: a.reshape(1, -1)
    x2d = x.reshape(batch * seq, d_model)
    for l in range(depth):
        w1, wq, wkv = _layer_weights(w_in[l], w_uq[l], w_ukv[l])
        pa, pb, qc, kc, vc = _projection(x2d, w1, row(q_norm_g[l]), row(kv_norm_g[l]), wq, wkv,
                                         cos_t, sin_t, seq)
        lam_init = 0.8 - 0.6 * math.exp(-0.3 * l)
        oa = _attention_a(pa.reshape(batch, seq, -1), bias_a)
        ob = _attention_b(pb.reshape(batch, seq, -1), bias_b, diff_lambda[l],
                          row(jnp.concatenate([subln_g[l], subln_g[l]])), lam_init)
        oc = _attention_c(qc.reshape(batch, seq, -1), kc.reshape(batch, seq, -1),
                          vc.reshape(batch, seq, -1))
        flat = lambda a: a.reshape(batch * seq, -1)
        x2d = _out_ffn(x2d, flat(oa), flat(ob), flat(oc), w_o[l].astype(BF16),
                       row(ln1_g[l]), row(ln1_b[l]), row(ln2_g[l]), row(ln2_b[l]),
                       w_gate[l].astype(BF16), w_up[l].astype(BF16), w_down[l].astype(BF16), alpha)
    return x2d.reshape(batch, seq, d_model)
```

```python
import functools
import math

import numpy as np
import jax
import jax.numpy as jnp
from jax import lax
from jax.experimental import pallas as pl
from jax.experimental.pallas import tpu as pltpu

F32 = jnp.float32
BF16 = jnp.bfloat16

D_MODEL = 1024
HEAD_DIM = 64
A_HEADS = 6
A_PAIRS = ((128, 1), (512, 4), (2048, 16))
A_BLK = 128
A_PAIR0_UNROLL = 5
B_HEADS = 4
B_QK_DIM = 32
C_HEADS = 6
C_Q_LORA = 256
C_KV_LORA = 128
C_NOPE = 64
C_ROPE = 32
ROPE_THETA = 10000.0
A_COLS = 3 * A_HEADS * HEAD_DIM
B_COLS = B_HEADS * 3 * HEAD_DIM
NUM_BUCKETS = 32
MAX_DISTANCE = 2048
FF_DIM = 2816
LN_EPS = 1e-5
LATENT_EPS = 1e-6
SUBLN_EPS = 1e-5

LANES = 128
MASK_VALUE = -1e30
LOG2E = math.log2(math.e)
A_QSCALE = HEAD_DIM ** -0.5 * LOG2E
B_QSCALE = B_QK_DIM ** -0.5 * LOG2E
C_QSCALE = (C_NOPE + C_ROPE) ** -0.5 * LOG2E
VMEM_LIMIT = 56 * 1024 * 1024

PROJ_TM = 512
FFN_TM = 512
MXU_WIDTH = 256
FFN_CHUNK_EDGES = (0, 5 * MXU_WIDTH, FF_DIM)
BQ = 512
BK = 512
QK_RATIO = BQ // BK


def _bucket_lower_bounds():
    max_exact = NUM_BUCKETS // 2
    d = np.arange(0, MAX_DISTANCE + 1)
    val = np.log(np.maximum(d, 1) / max_exact) / math.log(MAX_DISTANCE / max_exact) * (NUM_BUCKETS - max_exact)
    large = np.minimum(max_exact + np.floor(np.maximum(val, 0.0)).astype(np.int64), NUM_BUCKETS - 1)
    bucket = np.where(d < max_exact, d, large)
    return [int(np.argmax(bucket >= b)) for b in range(NUM_BUCKETS)]


_BUCKET_LO = _bucket_lower_bounds()


def _nt_dot(a, b):
    return lax.dot_general(a, b, (((1,), (1,)), ((), ())), preferred_element_type=F32)


def _own_head_lanes(rows):
    lane = lax.broadcasted_iota(jnp.int32, (rows, LANES), 1)
    return lane < HEAD_DIM, lane >= HEAD_DIM


def _values_and_ones(v, own):
    return jnp.where(own, v, jnp.ones_like(v))


def _divide_by_row_sum(acc):
    return acc / pltpu.roll(acc, HEAD_DIM, 1)


def _bias_tile_kernel(rb_ref, out_ref, *, head_offset, dilated, g_shift):
    h = pl.program_id(0) + head_offset
    g = pl.program_id(1)
    rows, cols = out_ref.shape[2], out_ref.shape[3]
    i = lax.broadcasted_iota(jnp.int32, (rows, cols), 0)
    j = lax.broadcasted_iota(jnp.int32, (rows, cols), 1)
    if dilated:
        rel = i + A_BLK - j
        valid = (rel >= 0) & (rel <= A_BLK)
        dil = jnp.where(g == 0, A_PAIRS[0][1], jnp.where(g == 1, A_PAIRS[1][1], A_PAIRS[2][1]))
        dist = rel * dil
    else:
        dist = (g - g_shift) * cols + i - j
        valid = dist >= 0
    val = jnp.full((rows, cols), rb_ref[0, h], F32)
    for b in range(1, NUM_BUCKETS):
        val = jnp.where(dist >= _BUCKET_LO[b], rb_ref[b, h], val)
    out_ref[0, 0] = jnp.where(valid, val * LOG2E, MASK_VALUE)


def _bias_tiles(rel_bias, n_heads, head_offset, n_groups, rows, cols, dilated, g_shift=0):
    return pl.pallas_call(
        functools.partial(_bias_tile_kernel, head_offset=head_offset, dilated=dilated, g_shift=g_shift),
        grid=(n_heads, n_groups),
        in_specs=[pl.BlockSpec(memory_space=pltpu.SMEM)],
        out_specs=pl.BlockSpec((1, 1, rows, cols), lambda h, g: (h, g, 0, 0)),
        out_shape=jax.ShapeDtypeStruct((n_heads, n_groups, rows, cols), F32),
        name="bias_tiles",
    )(rel_bias)


def _proj_kernel(x_ref, w1_ref, qg_ref, kvg_ref, wq_ref, wkv_ref, cos_ref, sin_ref,
                 pa_ref, pb_ref, qc_ref, kc_ref, vc_ref):
    xb = x_ref[...].astype(BF16)
    proj = jnp.dot(xb, w1_ref[...], preferred_element_type=F32)
    a_q = A_HEADS * HEAD_DIM
    pa_ref[:, :a_q] = proj[:, :a_q] * A_QSCALE
    pa_ref[:, a_q:] = proj[:, a_q:A_COLS]
    b_q = B_HEADS * HEAD_DIM
    pb_ref[:, :b_q] = (proj[:, A_COLS:A_COLS + b_q] * B_QSCALE).astype(BF16)
    pb_ref[:, b_q:] = proj[:, A_COLS + b_q:A_COLS + B_COLS].astype(BF16)
    c = proj[:, A_COLS + B_COLS:]
    c_q = c[:, :C_Q_LORA]
    c_kv = c[:, C_Q_LORA:C_Q_LORA + C_KV_LORA]
    k_rot = c[:, C_Q_LORA + C_KV_LORA:C_Q_LORA + C_KV_LORA + LANES]
    k_rot_half = c[:, C_Q_LORA + C_KV_LORA + LANES:]
    cqn = (c_q * lax.rsqrt(jnp.mean(c_q * c_q, axis=-1, keepdims=True) + LATENT_EPS) * qg_ref[...]).astype(BF16)
    ckvn = (c_kv * lax.rsqrt(jnp.mean(c_kv * c_kv, axis=-1, keepdims=True) + LATENT_EPS) * kvg_ref[...]).astype(BF16)
    q2 = jnp.dot(cqn, wq_ref[...], preferred_element_type=F32)
    kv = jnp.dot(ckvn, wkv_ref[...], preferred_element_type=F32)
    cos = cos_ref[...]
    sin = sin_ref[...]
    k_rope = k_rot * cos + k_rot_half * sin
    qw = C_HEADS * LANES
    for h in range(C_HEADS):
        sl = slice(h * LANES, (h + 1) * LANES)
        sl2 = slice(qw + h * LANES, qw + (h + 1) * LANES)
        qc_ref[:, sl] = ((q2[:, sl] * cos + q2[:, sl2] * sin) * C_QSCALE).astype(BF16)
        kc_ref[:, sl] = (kv[:, sl] + k_rope).astype(BF16)
    vc_ref[...] = kv[:, qw:].astype(BF16)


def _projection(x2d, w1, qg, kvg, wq, wkv, cos_t, sin_t, seq):
    n_tok = x2d.shape[0]
    tm = PROJ_TM
    pos_blocks = seq // tm
    full = lambda shape: pl.BlockSpec(shape, lambda i: (0, 0))
    tok = lambda width: pl.BlockSpec((tm, width), lambda i: (i, 0))
    pos = pl.BlockSpec((tm, LANES), lambda i: (i % pos_blocks, 0))
    return pl.pallas_call(
        _proj_kernel,
        grid=(n_tok // tm,),
        in_specs=[tok(D_MODEL), full(w1.shape), full(qg.shape), full(kvg.shape), full(wq.shape),
                  full(wkv.shape), pos, pos],
        out_specs=[tok(A_COLS), tok(B_COLS), tok(C_HEADS * LANES), tok(C_HEADS * LANES),
                   tok(C_HEADS * HEAD_DIM)],
        out_shape=[jax.ShapeDtypeStruct((n_tok, A_COLS), F32),
                   jax.ShapeDtypeStruct((n_tok, B_COLS), BF16),
                   jax.ShapeDtypeStruct((n_tok, C_HEADS * LANES), BF16),
                   jax.ShapeDtypeStruct((n_tok, C_HEADS * LANES), BF16),
                   jax.ShapeDtypeStruct((n_tok, C_HEADS * HEAD_DIM), BF16)],
        compiler_params=pltpu.CompilerParams(dimension_semantics=("arbitrary",),
                                             vmem_limit_bytes=VMEM_LIMIT),
        name="projection",
    )(x2d, w1, qg, kvg, wq, wkv, cos_t, sin_t)


def _attn_a_kernel(q_ref, k_ref, v_ref, bias_ref, o_ref, d4_ref, d16_ref, r0_ref, r1c_ref, r1_ref,
                   r2c_ref, r2m_ref, r2_ref):
    seq = q_ref.shape[1]
    len4, len16 = seq // 4, seq // 16
    lane = lax.broadcasted_iota(jnp.int32, (A_BLK, LANES), 1)
    lo_half = lane < HEAD_DIM
    ones = jnp.ones((2 * A_BLK, LANES), BF16)

    for a, src in enumerate((q_ref, k_ref, v_ref)):
        for r4 in range(4):
            d4_ref[a, r4] = src[0, pl.ds(r4, len4, stride=4), :]
        for r4 in range(4):
            for hi in range(4):
                d16_ref[a, 4 * hi + r4] = d4_ref[a, r4, pl.ds(hi, len16, stride=4), :]

    def attend(qf, kf, vf, p, first):
        nk = kf.shape[0]
        kb = kf.astype(BF16)
        v_aug = jnp.concatenate([vf.astype(BF16), ones[:nk]], axis=1)
        out, lse = [], []
        for j in range(2):
            qj = jnp.where(lo_half if j == 0 else jnp.logical_not(lo_half), qf, 0.0).astype(BF16)
            bias = bias_ref[j, p]
            if first:
                bias = bias[:, A_BLK:]
            s = _nt_dot(qj, kb) + bias
            m = jnp.max(s, axis=-1, keepdims=True)
            pr = jnp.exp2(s - m).astype(BF16)
            r = jnp.dot(pr, v_aug, preferred_element_type=F32)
            den = r[:, LANES:]
            out.append(r[:, :LANES] / den)
            lse.append(m + jnp.log2(den))
        return jnp.where(lo_half, out[0], out[1]), jnp.where(lo_half, lse[0], lse[1])

    def key_rows(start, first):
        return pl.ds(start, A_BLK) if first else pl.ds(start - A_BLK, 2 * A_BLK)

    def pair0_block(start, first):
        kr = key_rows(start, first)
        out, lse = attend(q_ref[0, pl.ds(start, A_BLK), :], k_ref[0, kr, :], v_ref[0, kr, :], 0, first)
        r0_ref[0, pl.ds(start, A_BLK), :] = out
        r0_ref[1, pl.ds(start, A_BLK), :] = lse

    pair0_block(0, True)

    def pair0_later(n, c):
        pair0_block(pl.multiple_of(n * A_BLK, A_BLK), False)
        return c
    lax.fori_loop(1, seq // A_BLK, pair0_later, 0, unroll=A_PAIR0_UNROLL)

    def pair1_class(r4, c):
        for n in range(len4 // A_BLK):
            kr = key_rows(n * A_BLK, n == 0)
            out, lse = attend(d4_ref[0, r4, pl.ds(n * A_BLK, A_BLK), :], d4_ref[1, r4, kr, :],
                              d4_ref[2, r4, kr, :], 1, n == 0)
            r1c_ref[0, r4, pl.ds(n * A_BLK, A_BLK), :] = out
            r1c_ref[1, r4, pl.ds(n * A_BLK, A_BLK), :] = lse
        return c
    for r4 in range(4):
        pair1_class(r4, 0)

    def pair2_class(r, c):
        out, lse = attend(d16_ref[0, r], d16_ref[1, r], d16_ref[2, r], 2, True)
        r2c_ref[0, r] = out
        r2c_ref[1, r] = lse
        return c
    for r in range(16):
        pair2_class(r, 0)

    for c in range(2):
        for r4 in range(4):
            for hi in range(4):
                r2m_ref[c, r4, pl.ds(hi, len16, stride=4), :] = r2c_ref[c, 4 * hi + r4]
        for r4 in range(4):
            r1_ref[c, pl.ds(r4, len4, stride=4), :] = r1c_ref[c, r4]
            r2_ref[c, pl.ds(r4, len4, stride=4), :] = r2m_ref[c, r4]

    chunk = 256

    def combine(i, carry):
        rs = pl.ds(pl.multiple_of(i * chunk, chunk), chunk)
        l0, l1, l2 = r0_ref[1, rs, :], r1_ref[1, rs, :], r2_ref[1, rs, :]
        mx = jnp.maximum(jnp.maximum(l0, l1), l2)
        e0, e1, e2 = jnp.exp2(l0 - mx), jnp.exp2(l1 - mx), jnp.exp2(l2 - mx)
        num = e0 * r0_ref[0, rs, :] + e1 * r1_ref[0, rs, :] + e2 * r2_ref[0, rs, :]
        o_ref[0, rs, :] = (num / (e0 + e1 + e2)).astype(o_ref.dtype)
        return carry

    lax.fori_loop(0, seq // chunk, combine, 0)


def _attention_a(proj_a, bias_a):
    batch, seq, _ = proj_a.shape
    pairs = A_HEADS // 2
    n_pairs = len(A_PAIRS)
    col = lambda off: pl.BlockSpec((1, seq, LANES), lambda hp, b: (b, 0, off + hp))
    result = pltpu.VMEM((2, seq, LANES), F32)
    by4 = pltpu.VMEM((2, 4, seq // 4, LANES), F32)
    return pl.pallas_call(
        _attn_a_kernel,
        grid=(pairs, batch),
        in_specs=[col(0), col(pairs), col(2 * pairs),
                  pl.BlockSpec((2, n_pairs, A_BLK, 2 * A_BLK), lambda hp, b: (hp, 0, 0, 0))],
        out_specs=pl.BlockSpec((1, seq, LANES), lambda hp, b: (b, 0, hp)),
        out_shape=jax.ShapeDtypeStruct((batch, seq, A_HEADS * HEAD_DIM), BF16),
        scratch_shapes=[pltpu.VMEM((3, 4, seq // 4, LANES), F32), pltpu.VMEM((3, 16, seq // 16, LANES), F32),
                        result, by4, result, pltpu.VMEM((2, 16, seq // 16, LANES), F32), by4, result],
        compiler_params=pltpu.CompilerParams(dimension_semantics=("arbitrary", "arbitrary"),
                                             vmem_limit_bytes=VMEM_LIMIT),
        name="attention_a",
    )(proj_a, proj_a, proj_a, bias_a)


def _softmax_init(m_ref, acc_ref):
    m_ref[...] = jnp.full(m_ref.shape, MASK_VALUE, F32)
    acc_ref[...] = jnp.zeros(acc_ref.shape, F32)


def _softmax_step(s, v_aug, m_ref, acc_ref, c):
    m_old = m_ref[c]
    m_new = jnp.max(jnp.concatenate([s, m_old], axis=1), axis=-1, keepdims=True)
    alpha = jnp.exp2(m_old - m_new)
    pr = jnp.exp2(s - m_new).astype(BF16)
    acc_ref[c] = acc_ref[c] * alpha + jnp.dot(pr, v_aug, preferred_element_type=F32)
    m_ref[c] = jnp.broadcast_to(m_new, m_old.shape)


def _attn_b_kernel(q_ref, k_ref, v_ref, bias_ref, lam_ref, g_ref, o_ref, m_ref, acc_ref, *, lam_init):
    qi = pl.program_id(2)
    q = q_ref[0]
    lane = lax.broadcasted_iota(jnp.int32, (BQ, LANES), 1)
    own = _own_head_lanes(BK)
    lf = lam_ref[...]
    dot_rows = lambda a, b: jnp.sum(lf[a:a + 1] * lf[b:b + 1], axis=-1, keepdims=True)
    lam = jnp.exp(dot_rows(0, 1)) - jnp.exp(dot_rows(2, 3)) + lam_init

    qs = [jnp.where((lane >= c * B_QK_DIM) & (lane < (c + 1) * B_QK_DIM), q, jnp.zeros_like(q))
          for c in range(4)]

    def step(ki, carry):
        off = pl.multiple_of(ki * BK, BK)
        k = k_ref[0, pl.ds(off, BK), :]
        v = v_ref[0, pl.ds(off, BK), :]
        v_aug = [_values_and_ones(v, own[j]) for j in range(2)]
        for c in range(4):
            bias = bias_ref[c // 2, QK_RATIO * qi + (QK_RATIO - 1) - ki]
            _softmax_step(_nt_dot(qs[c], k) + bias, v_aug[c // 2], m_ref, acc_ref, c)
        return carry

    _softmax_init(m_ref, acc_ref)
    lax.fori_loop(0, QK_RATIO * (qi + 1), step, 0)

    outs = []
    for j in range(2):
        base = j * HEAD_DIM
        a1, a2 = acc_ref[2 * j], acc_ref[2 * j + 1]
        o = _divide_by_row_sum(a1) - lam * _divide_by_row_sum(a2)
        in_head = (lane >= base) & (lane < base + HEAD_DIM)
        ms = jnp.sum(jnp.where(in_head, o * o, 0.0), axis=-1, keepdims=True) * (1.0 / HEAD_DIM)
        outs.append(o * lax.rsqrt(ms + SUBLN_EPS) * g_ref[...] * (1.0 - lam_init))
    o_ref[0] = jnp.where(lane < HEAD_DIM, outs[0], outs[1]).astype(o_ref.dtype)


def _attention_b(proj_b, bias_b, diff_lambda, subln_g2, lam_init):
    batch, seq, _ = proj_b.shape
    pairs = B_HEADS // 2
    n_tiles = seq // BQ
    return pl.pallas_call(
        functools.partial(_attn_b_kernel, lam_init=lam_init),
        grid=(pairs, batch, n_tiles),
        in_specs=[pl.BlockSpec((1, BQ, LANES), lambda hp, b, i: (b, i, hp)),
                  pl.BlockSpec((1, seq, LANES), lambda hp, b, i: (b, 0, pairs + hp)),
                  pl.BlockSpec((1, seq, LANES), lambda hp, b, i: (b, 0, 2 * pairs + hp)),
                  pl.BlockSpec((2, seq // BK, BQ, BK), lambda hp, b, i: (hp, 0, 0, 0),
                               pipeline_mode=pl.Buffered(1)),
                  pl.BlockSpec(diff_lambda.shape, lambda hp, b, i: (0, 0)),
                  pl.BlockSpec(subln_g2.shape, lambda hp, b, i: (0, 0))],
        out_specs=pl.BlockSpec((1, BQ, LANES), lambda hp, b, i: (b, i, hp)),
        out_shape=jax.ShapeDtypeStruct((batch, seq, B_HEADS * HEAD_DIM), BF16),
        scratch_shapes=[pltpu.VMEM((4, BQ, LANES), F32)] * 2,
        compiler_params=pltpu.CompilerParams(dimension_semantics=("arbitrary",) * 3,
                                             vmem_limit_bytes=VMEM_LIMIT),
        name="attention_b",
    )(proj_b, proj_b, proj_b, bias_b, diff_lambda, subln_g2)


def _attn_c_kernel(q_ref, k_ref, v_ref, o_ref, m_ref, acc_ref):
    qi = pl.program_id(2)
    lane = lax.broadcasted_iota(jnp.int32, (BQ, LANES), 1)
    own = _own_head_lanes(BK)
    row = lax.broadcasted_iota(jnp.int32, (BQ, BK), 0)
    colm = lax.broadcasted_iota(jnp.int32, (BQ, BK), 1)

    def step(ki, carry, masked):
        off = pl.multiple_of(ki * BK, BK)
        v = v_ref[0, pl.ds(off, BK), :]
        for j in range(2):
            hs = slice(j * LANES, (j + 1) * LANES)
            s = _nt_dot(q_ref[0, :, hs], k_ref[0, pl.ds(off, BK), hs])
            if masked is not None:
                s = s + jnp.where(row - colm >= masked * BK, 0.0, MASK_VALUE)
            _softmax_step(s, _values_and_ones(v, own[j]), m_ref, acc_ref, j)
        return carry

    _softmax_init(m_ref, acc_ref)
    lax.fori_loop(0, QK_RATIO * qi, functools.partial(step, masked=None), 0)
    for t in range(QK_RATIO):
        step(QK_RATIO * qi + t, 0, t)
    o_ref[0] = jnp.where(lane < HEAD_DIM, _divide_by_row_sum(acc_ref[0]),
                         _divide_by_row_sum(acc_ref[1])).astype(o_ref.dtype)


def _attention_c(qc, kc, vc):
    batch, seq, _ = qc.shape
    pairs = C_HEADS // 2
    n_tiles = seq // BQ
    return pl.pallas_call(
        _attn_c_kernel,
        grid=(pairs, batch, n_tiles),
        in_specs=[pl.BlockSpec((1, BQ, 2 * LANES), lambda hp, b, i: (b, i, hp)),
                  pl.BlockSpec((1, seq, 2 * LANES), lambda hp, b, i: (b, 0, hp)),
                  pl.BlockSpec((1, seq, LANES), lambda hp, b, i: (b, 0, hp))],
        out_specs=pl.BlockSpec((1, BQ, LANES), lambda hp, b, i: (b, i, hp)),
        out_shape=jax.ShapeDtypeStruct((batch, seq, C_HEADS * HEAD_DIM), BF16),
        scratch_shapes=[pltpu.VMEM((2, BQ, LANES), F32)] * 2,
        compiler_params=pltpu.CompilerParams(dimension_semantics=("arbitrary",) * 3,
                                             vmem_limit_bytes=VMEM_LIMIT),
        name="attention_c",
    )(qc, kc, vc)


def _layer_norm(y, g, b):
    mu = jnp.mean(y, axis=-1, keepdims=True)
    yc = y - mu
    var = jnp.mean(yc * yc, axis=-1, keepdims=True)
    return yc * lax.rsqrt(var + LN_EPS) * g + b


def _out_ffn_kernel(x_ref, oa_ref, ob_ref, oc_ref, wo_ref, g1_ref, b1_ref, g2_ref, b2_ref,
                    wg_ref, wu_ref, wd_ref, out_ref, *, alpha):
    heads = jnp.concatenate([oa_ref[...], ob_ref[...], oc_ref[...]], axis=1)
    mix = jnp.dot(heads, wo_ref[...], preferred_element_type=F32)
    h = _layer_norm(alpha * x_ref[...] + mix, g1_ref[...], b1_ref[...])
    hb = h.astype(BF16)
    ffn = jnp.zeros(h.shape, F32)
    for lo, hi in zip(FFN_CHUNK_EDGES[:-1], FFN_CHUNK_EDGES[1:]):
        cs = slice(lo, hi)
        gate = jnp.dot(hb, wg_ref[:, cs], preferred_element_type=F32)
        up = jnp.dot(hb, wu_ref[:, cs], preferred_element_type=F32)
        act = (jax.nn.silu(gate) * up).astype(BF16)
        ffn = ffn + jnp.dot(act, wd_ref[cs, :], preferred_element_type=F32)
    out_ref[...] = _layer_norm(alpha * h + ffn, g2_ref[...], b2_ref[...])


def _out_ffn(x2d, oa, ob, oc, wo, g1, b1, g2, b2, wg, wu, wd, alpha):
    n_tok = x2d.shape[0]
    tm = FFN_TM
    tok = lambda width: pl.BlockSpec((tm, width), lambda i: (i, 0))
    once = lambda a: pl.BlockSpec(a.shape, lambda i: (0, 0), pipeline_mode=pl.Buffered(1))
    return pl.pallas_call(
        functools.partial(_out_ffn_kernel, alpha=alpha),
        grid=(n_tok // tm,),
        in_specs=[tok(D_MODEL), tok(oa.shape[1]), tok(ob.shape[1]), tok(oc.shape[1]),
                  once(wo), once(g1), once(b1), once(g2), once(b2), once(wg), once(wu), once(wd)],
        out_specs=tok(D_MODEL),
        out_shape=jax.ShapeDtypeStruct((n_tok, D_MODEL), F32),
        compiler_params=pltpu.CompilerParams(dimension_semantics=("arbitrary",),
                                             vmem_limit_bytes=VMEM_LIMIT),
        name="out_ffn",
    )(x2d, oa, ob, oc, wo, g1, b1, g2, b2, wg, wu, wd)


def _rotate_half_cols(w):
    half = w.shape[-1] // 2
    return jnp.concatenate([-w[..., half:], w[..., :half]], axis=-1)


def _place_rope(w):
    return jnp.pad(w, ((0, 0), (C_NOPE, LANES - C_NOPE - C_ROPE)))


def _layer_weights(w_in, w_uq, w_ukv):
    main = A_COLS + B_COLS + C_Q_LORA + C_KV_LORA
    k_r = w_in[:, main:]
    w1 = jnp.concatenate([w_in[:, :main], _place_rope(k_r), _place_rope(_rotate_half_cols(k_r))],
                         axis=1).astype(BF16)
    rows = w_uq.shape[0]
    uq = w_uq.reshape(rows, C_HEADS, C_NOPE + C_ROPE)
    nope, rope = uq[..., :C_NOPE], uq[..., C_NOPE:]
    tail = jnp.zeros((rows, C_HEADS, LANES - C_NOPE - C_ROPE), w_uq.dtype)
    q_main = jnp.concatenate([nope, rope, tail], axis=-1).reshape(rows, C_HEADS * LANES)
    q_half = jnp.concatenate([jnp.zeros_like(nope), _rotate_half_cols(rope), tail],
                             axis=-1).reshape(rows, C_HEADS * LANES)
    wq = jnp.concatenate([q_main, q_half], axis=1).astype(BF16)
    rows = w_ukv.shape[0]
    ukv = w_ukv.reshape(rows, C_HEADS, C_NOPE + HEAD_DIM)
    k_nope, v = ukv[..., :C_NOPE], ukv[..., C_NOPE:]
    k_main = jnp.concatenate([k_nope, jnp.zeros((rows, C_HEADS, LANES - C_NOPE), w_ukv.dtype)],
                             axis=-1).reshape(rows, C_HEADS * LANES)
    wkv = jnp.concatenate([k_main, v.reshape(rows, C_HEADS * HEAD_DIM)], axis=1).astype(BF16)
    return w1, wq, wkv


def _rope_tables(seq):
    half = C_ROPE // 2
    inv = ROPE_THETA ** (-np.arange(half, dtype=np.float64) / half)
    ang = np.arange(seq, dtype=np.float64)[:, None] * inv[None, :]
    cos = np.zeros((seq, LANES), np.float64)
    sin = np.zeros((seq, LANES), np.float64)
    cos[:, :C_NOPE] = 1.0
    cos[:, C_NOPE:C_NOPE + C_ROPE] = np.concatenate([np.cos(ang), np.cos(ang)], axis=1)
    sin[:, C_NOPE:C_NOPE + C_ROPE] = np.concatenate([np.sin(ang), np.sin(ang)], axis=1)
    return jnp.asarray(cos, F32), jnp.asarray(sin, F32)


def kernel(x, rel_bias, w_in, q_norm_g, kv_norm_g, w_uq, w_ukv, diff_lambda, subln_g, w_o, ln1_g, ln1_b, ln2_g, ln2_b, w_gate, w_up, w_down):
    batch, seq, d_model = x.shape
    depth = w_in.shape[0]
    alpha = (2 * depth) ** 0.25
    bias_a = _bias_tiles(rel_bias, A_HEADS, 0, len(A_PAIRS), A_BLK, 2 * A_BLK, True)
    bias_b = _bias_tiles(rel_bias, B_HEADS, A_HEADS, seq // BK, BQ, BK, False, QK_RATIO - 1)
    cos_t, sin_t = _rope_tables(seq)
    row = lambda a: a.reshape(1, -1)
    x2d = x.reshape(batch * seq, d_model)
    for l in range(depth):
        w1, wq, wkv = _layer_weights(w_in[l], w_uq[l], w_ukv[l])
        pa, pb, qc, kc, vc = _projection(x2d, w1, row(q_norm_g[l]), row(kv_norm_g[l]), wq, wkv,
                                         cos_t, sin_t, seq)
        lam_init = 0.8 - 0.6 * math.exp(-0.3 * l)
        oa = _attention_a(pa.reshape(batch, seq, -1), bias_a)
        ob = _attention_b(pb.reshape(batch, seq, -1), bias_b, diff_lambda[l],
                          row(jnp.concatenate([subln_g[l], subln_g[l]])), lam_init)
        oc = _attention_c(qc.reshape(batch, seq, -1), kc.reshape(batch, seq, -1),
                          vc.reshape(batch, seq, -1))
        flat = lambda a: a.reshape(batch * seq, -1)
        x2d = _out_ffn(x2d, flat(oa), flat(ob), flat(oc), w_o[l].astype(BF16),
                       row(ln1_g[l]), row(ln1_b[l]), row(ln2_g[l]), row(ln2_b[l]),
                       w_gate[l].astype(BF16), w_up[l].astype(BF16), w_down[l].astype(BF16), alpha)
    return x2d.reshape(batch, seq, d_model)
```

```python
import functools
import math

import numpy as np
import jax
import jax.numpy as jnp
from jax import lax
from jax.experimental import pallas as pl
from jax.experimental.pallas import tpu as pltpu

F32 = jnp.float32
BF16 = jnp.bfloat16

D_MODEL = 1024
HEAD_DIM = 64
A_HEADS = 6
A_PAIRS = ((128, 1), (512, 4), (2048, 16))
A_BLK = 128
A_PAIR0_UNROLL = 5
B_HEADS = 4
B_QK_DIM = 32
C_HEADS = 6
C_Q_LORA = 256
C_KV_LORA = 128
C_NOPE = 64
C_ROPE = 32
ROPE_THETA = 10000.0
A_COLS = 3 * A_HEADS * HEAD_DIM
B_COLS = B_HEADS * 3 * HEAD_DIM
NUM_BUCKETS = 32
MAX_DISTANCE = 2048
FF_DIM = 2816
LN_EPS = 1e-5
LATENT_EPS = 1e-6
SUBLN_EPS = 1e-5

LANES = 128
MASK_VALUE = -1e30
LOG2E = math.log2(math.e)
A_QSCALE = HEAD_DIM ** -0.5 * LOG2E
B_QSCALE = B_QK_DIM ** -0.5 * LOG2E
C_QSCALE = (C_NOPE + C_ROPE) ** -0.5 * LOG2E
VMEM_LIMIT = 56 * 1024 * 1024

PROJ_TM = 512
FFN_TM = 512
MXU_WIDTH = 256
FFN_CHUNK_EDGES = (0, 5 * MXU_WIDTH, FF_DIM)
BQ = 512
BK = 512
QK_RATIO = BQ // BK


def _bucket_lower_bounds():
    max_exact = NUM_BUCKETS // 2
    d = np.arange(0, MAX_DISTANCE + 1)
    val = np.log(np.maximum(d, 1) / max_exact) / math.log(MAX_DISTANCE / max_exact) * (NUM_BUCKETS - max_exact)
    large = np.minimum(max_exact + np.floor(np.maximum(val, 0.0)).astype(np.int64), NUM_BUCKETS - 1)
    bucket = np.where(d < max_exact, d, large)
    return [int(np.argmax(bucket >= b)) for b in range(NUM_BUCKETS)]


_BUCKET_LO = _bucket_lower_bounds()


def _nt_dot(a, b):
    return lax.dot_general(a, b, (((1,), (1,)), ((), ())), preferred_element_type=F32)


def _own_head_lanes(rows):
    lane = lax.broadcasted_iota(jnp.int32, (rows, LANES), 1)
    return lane < HEAD_DIM, lane >= HEAD_DIM


def _values_and_ones(v, own):
    return jnp.where(own, v, jnp.ones_like(v))


def _divide_by_row_sum(acc):
    return acc / pltpu.roll(acc, HEAD_DIM, 1)


def _bias_tile_kernel(rb_ref, out_ref, *, head_offset, dilated, g_shift):
    h = pl.program_id(0) + head_offset
    g = pl.program_id(1)
    rows, cols = out_ref.shape[2], out_ref.shape[3]
    i = lax.broadcasted_iota(jnp.int32, (rows, cols), 0)
    j = lax.broadcasted_iota(jnp.int32, (rows, cols), 1)
    if dilated:
        rel = i + A_BLK - j
        valid = (rel >= 0) & (rel <= A_BLK)
        dil = jnp.where(g == 0, A_PAIRS[0][1], jnp.where(g == 1, A_PAIRS[1][1], A_PAIRS[2][1]))
        dist = rel * dil
    else:
        dist = (g - g_shift) * cols + i - j
        valid = dist >= 0
    val = jnp.full((rows, cols), rb_ref[0, h], F32)
    for b in range(1, NUM_BUCKETS):
        val = jnp.where(dist >= _BUCKET_LO[b], rb_ref[b, h], val)
    out_ref[0, 0] = jnp.where(valid, val * LOG2E, MASK_VALUE)


def _bias_tiles(rel_bias, n_heads, head_offset, n_groups, rows, cols, dilated, g_shift=0):
    return pl.pallas_call(
        functools.partial(_bias_tile_kernel, head_offset=head_offset, dilated=dilated, g_shift=g_shift),
        grid=(n_heads, n_groups),
        in_specs=[pl.BlockSpec(memory_space=pltpu.SMEM)],
        out_specs=pl.BlockSpec((1, 1, rows, cols), lambda h, g: (h, g, 0, 0)),
        out_shape=jax.ShapeDtypeStruct((n_heads, n_groups, rows, cols), F32),
        name="bias_tiles",
    )(rel_bias)


def _proj_kernel(x_ref, w1_ref, qg_ref, kvg_ref, wq_ref, wkv_ref, cos_ref, sin_ref,
                 pa_ref, pb_ref, qc_ref, kc_ref, vc_ref):
    xb = x_ref[...].astype(BF16)
    proj = jnp.dot(xb, w1_ref[...], preferred_element_type=F32)
    a_q = A_HEADS * HEAD_DIM
    pa_ref[:, :a_q] = proj[:, :a_q] * A_QSCALE
    pa_ref[:, a_q:] = proj[:, a_q:A_COLS]
    b_q = B_HEADS * HEAD_DIM
    pb_ref[:, :b_q] = (proj[:, A_COLS:A_COLS + b_q] * B_QSCALE).astype(BF16)
    pb_ref[:, b_q:] = proj[:, A_COLS + b_q:A_COLS + B_COLS].astype(BF16)
    c = proj[:, A_COLS + B_COLS:]
    c_q = c[:, :C_Q_LORA]
    c_kv = c[:, C_Q_LORA:C_Q_LORA + C_KV_LORA]
    k_rot = c[:, C_Q_LORA + C_KV_LORA:C_Q_LORA + C_KV_LORA + LANES]
    k_rot_half = c[:, C_Q_LORA + C_KV_LORA + LANES:]
    cqn = (c_q * lax.rsqrt(jnp.mean(c_q * c_q, axis=-1, keepdims=True) + LATENT_EPS) * qg_ref[...]).astype(BF16)
    ckvn = (c_kv * lax.rsqrt(jnp.mean(c_kv * c_kv, axis=-1, keepdims=True) + LATENT_EPS) * kvg_ref[...]).astype(BF16)
    q2 = jnp.dot(cqn, wq_ref[...], preferred_element_type=F32)
    kv = jnp.dot(ckvn, wkv_ref[...], preferred_element_type=F32)
    cos = cos_ref[...]
    sin = sin_ref[...]
    k_rope = k_rot * cos + k_rot_half * sin
    qw = C_HEADS * LANES
    for h in range(C_HEADS):
        sl = slice(h * LANES, (h + 1) * LANES)
        sl2 = slice(qw + h * LANES, qw + (h + 1) * LANES)
        qc_ref[:, sl] = ((q2[:, sl] * cos + q2[:, sl2] * sin) * C_QSCALE).astype(BF16)
        kc_ref[:, sl] = (kv[:, sl] + k_rope).astype(BF16)
    vc_ref[...] = kv[:, qw:].astype(BF16)


def _projection(x2d, w1, qg, kvg, wq, wkv, cos_t, sin_t, seq):
    n_tok = x2d.shape[0]
    tm = PROJ_TM
    pos_blocks = seq // tm
    full = lambda shape: pl.BlockSpec(shape, lambda i: (0, 0))
    tok = lambda width: pl.BlockSpec((tm, width), lambda i: (i, 0))
    pos = pl.BlockSpec((tm, LANES), lambda i: (i % pos_blocks, 0))
    return pl.pallas_call(
        _proj_kernel,
        grid=(n_tok // tm,),
        in_specs=[tok(D_MODEL), full(w1.shape), full(qg.shape), full(kvg.shape), full(wq.shape),
                  full(wkv.shape), pos, pos],
        out_specs=[tok(A_COLS), tok(B_COLS), tok(C_HEADS * LANES), tok(C_HEADS * LANES),
                   tok(C_HEADS * HEAD_DIM)],
        out_shape=[jax.ShapeDtypeStruct((n_tok, A_COLS), F32),
                   jax.ShapeDtypeStruct((n_tok, B_COLS), BF16),
                   jax.ShapeDtypeStruct((n_tok, C_HEADS * LANES), BF16),
                   jax.ShapeDtypeStruct((n_tok, C_HEADS * LANES), BF16),
                   jax.ShapeDtypeStruct((n_tok, C_HEADS * HEAD_DIM), BF16)],
        compiler_params=pltpu.CompilerParams(dimension_semantics=("arbitrary",),
                                             vmem_limit_bytes=VMEM_LIMIT),
        name="projection",
    )(x2d, w1, qg, kvg, wq, wkv, cos_t, sin_t)


def _attn_a_kernel(q_ref, k_ref, v_ref, bias_ref, o_ref, d4_ref, d16_ref, r0_ref, r1c_ref, r1_ref,
                   r2c_ref, r2m_ref, r2_ref):
    seq = q_ref.shape[1]
    len4, len16 = seq // 4, seq // 16
    lane = lax.broadcasted_iota(jnp.int32, (A_BLK, LANES), 1)
    lo_half = lane < HEAD_DIM
    ones = jnp.ones((2 * A_BLK, LANES), BF16)

    for a, src in enumerate((q_ref, k_ref, v_ref)):
        for r4 in range(4):
            d4_ref[a, r4] = src[0, pl.ds(r4, len4, stride=4), :]
        for r4 in range(4):
            for hi in range(4):
                d16_ref[a, 4 * hi + r4] = d4_ref[a, r4, pl.ds(hi, len16, stride=4), :]

    def attend(qf, kf, vf, p, first):
        nk = kf.shape[0]
        kb = kf.astype(BF16)
        v_aug = jnp.concatenate([vf.astype(BF16), ones[:nk]], axis=1)
        out, lse = [], []
        for j in range(2):
            qj = jnp.where(lo_half if j == 0 else jnp.logical_not(lo_half), qf, 0.0).astype(BF16)
            bias = bias_ref[j, p]
            if first:
                bias = bias[:, A_BLK:]
            s = _nt_dot(qj, kb) + bias
            m = jnp.max(s, axis=-1, keepdims=True)
            pr = jnp.exp2(s - m).astype(BF16)
            r = jnp.dot(pr, v_aug, preferred_element_type=F32)
            den = r[:, LANES:]
            out.append(r[:, :LANES] / den)
            lse.append(m + jnp.log2(den))
        return jnp.where(lo_half, out[0], out[1]), jnp.where(lo_half, lse[0], lse[1])

    def key_rows(start, first):
        return pl.ds(start, A_BLK) if first else pl.ds(start - A_BLK, 2 * A_BLK)

    def pair0_block(start, first):
        kr = key_rows(start, first)
        out, lse = attend(q_ref[0, pl.ds(start, A_BLK), :], k_ref[0, kr, :], v_ref[0, kr, :], 0, first)
        r0_ref[0, pl.ds(start, A_BLK), :] = out
        r0_ref[1, pl.ds(start, A_BLK), :] = lse

    pair0_block(0, True)

    def pair0_later(n, c):
        pair0_block(pl.multiple_of(n * A_BLK, A_BLK), False)
        return c
    lax.fori_loop(1, seq // A_BLK, pair0_later, 0, unroll=A_PAIR0_UNROLL)

    def pair1_class(r4, c):
        for n in range(len4 // A_BLK):
            kr = key_rows(n * A_BLK, n == 0)
            out, lse = attend(d4_ref[0, r4, pl.ds(n * A_BLK, A_BLK), :], d4_ref[1, r4, kr, :],
                              d4_ref[2, r4, kr, :], 1, n == 0)
            r1c_ref[0, r4, pl.ds(n * A_BLK, A_BLK), :] = out
            r1c_ref[1, r4, pl.ds(n * A_BLK, A_BLK), :] = lse
        return c
    for r4 in range(4):
        pair1_class(r4, 0)

    def pair2_class(r, c):
        out, lse = attend(d16_ref[0, r], d16_ref[1, r], d16_ref[2, r], 2, True)
        r2c_ref[0, r] = out
        r2c_ref[1, r] = lse
        return c
    for r in range(16):
        pair2_class(r, 0)

    for c in range(2):
        for r4 in range(4):
            for hi in range(4):
                r2m_ref[c, r4, pl.ds(hi, len16, stride=4), :] = r2c_ref[c, 4 * hi + r4]
        for r4 in range(4):
            r1_ref[c, pl.ds(r4, len4, stride=4), :] = r1c_ref[c, r4]
            r2_ref[c, pl.ds(r4, len4, stride=4), :] = r2m_ref[c, r4]

    chunk = 256

    def combine(i, carry):
        rs = pl.ds(pl.multiple_of(i * chunk, chunk), chunk)
        l0, l1, l2 = r0_ref[1, rs, :], r1_ref[1, rs, :], r2_ref[1, rs, :]
        mx = jnp.maximum(jnp.maximum(l0, l1), l2)
        e0, e1, e2 = jnp.exp2(l0 - mx), jnp.exp2(l1 - mx), jnp.exp2(l2 - mx)
        num = e0 * r0_ref[0, rs, :] + e1 * r1_ref[0, rs, :] + e2 * r2_ref[0, rs, :]
        o_ref[0, rs, :] = (num / (e0 + e1 + e2)).astype(o_ref.dtype)
        return carry

    lax.fori_loop(0, seq // chunk, combine, 0)


def _attention_a(proj_a, bias_a):
    batch, seq, _ = proj_a.shape
    pairs = A_HEADS // 2
    n_pairs = len(A_PAIRS)
    col = lambda off: pl.BlockSpec((1, seq, LANES), lambda hp, b: (b, 0, off + hp))
    result = pltpu.VMEM((2, seq, LANES), F32)
    by4 = pltpu.VMEM((2, 4, seq // 4, LANES), F32)
    return pl.pallas_call(
        _attn_a_kernel,
        grid=(pairs, batch),
        in_specs=[col(0), col(pairs), col(2 * pairs),
                  pl.BlockSpec((2, n_pairs, A_BLK, 2 * A_BLK), lambda hp, b: (hp, 0, 0, 0))],
        out_specs=pl.BlockSpec((1, seq, LANES), lambda hp, b: (b, 0, hp)),
        out_shape=jax.ShapeDtypeStruct((batch, seq, A_HEADS * HEAD_DIM), BF16),
        scratch_shapes=[pltpu.VMEM((3, 4, seq // 4, LANES), F32), pltpu.VMEM((3, 16, seq // 16, LANES), F32),
                        result, by4, result, pltpu.VMEM((2, 16, seq // 16, LANES), F32), by4, result],
        compiler_params=pltpu.CompilerParams(dimension_semantics=("arbitrary", "arbitrary"),
                                             vmem_limit_bytes=VMEM_LIMIT),
        name="attention_a",
    )(proj_a, proj_a, proj_a, bias_a)


def _softmax_init(m_ref, acc_ref):
    m_ref[...] = jnp.full(m_ref.shape, MASK_VALUE, F32)
    acc_ref[...] = jnp.zeros(acc_ref.shape, F32)


def _softmax_step(s, v_aug, m_ref, acc_ref, c):
    m_old = m_ref[c]
    m_new = jnp.max(jnp.concatenate([s, m_old], axis=1), axis=-1, keepdims=True)
    alpha = jnp.exp2(m_old - m_new)
    pr = jnp.exp2(s - m_new).astype(BF16)
    acc_ref[c] = acc_ref[c] * alpha + jnp.dot(pr, v_aug, preferred_element_type=F32)
    m_ref[c] = jnp.broadcast_to(m_new, m_old.shape)


def _attn_b_kernel(q_ref, k_ref, v_ref, bias_ref, lam_ref, g_ref, o_ref, m_ref, acc_ref, *, lam_init):
    qi = pl.program_id(2)
    q = q_ref[0]
    lane = lax.broadcasted_iota(jnp.int32, (BQ, LANES), 1)
    own = _own_head_lanes(BK)
    lf = lam_ref[...]
    dot_rows = lambda a, b: jnp.sum(lf[a:a + 1] * lf[b:b + 1], axis=-1, keepdims=True)
    lam = jnp.exp(dot_rows(0, 1)) - jnp.exp(dot_rows(2, 3)) + lam_init

    qs = jnp.concatenate(
        [jnp.where((lane >= c * B_QK_DIM) & (lane < (c + 1) * B_QK_DIM), q, jnp.zeros_like(q))
         for c in range(4)], axis=0)

    def step(ki, carry):
        off = pl.multiple_of(ki * BK, BK)
        k = k_ref[0, pl.ds(off, BK), :]
        v = v_ref[0, pl.ds(off, BK), :]
        v_aug = [_values_and_ones(v, own[j]) for j in range(2)]
        s_all = _nt_dot(qs, k)
        for c in range(4):
            bias = bias_ref[c // 2, QK_RATIO * qi + (QK_RATIO - 1) - ki]
            _softmax_step(s_all[c * BQ:(c + 1) * BQ] + bias, v_aug[c // 2], m_ref, acc_ref, c)
        return carry

    _softmax_init(m_ref, acc_ref)
    lax.fori_loop(0, QK_RATIO * (qi + 1), step, 0)

    outs = []
    for j in range(2):
        base = j * HEAD_DIM
        a1, a2 = acc_ref[2 * j], acc_ref[2 * j + 1]
        o = _divide_by_row_sum(a1) - lam * _divide_by_row_sum(a2)
        in_head = (lane >= base) & (lane < base + HEAD_DIM)
        ms = jnp.sum(jnp.where(in_head, o * o, 0.0), axis=-1, keepdims=True) * (1.0 / HEAD_DIM)
        outs.append(o * lax.rsqrt(ms + SUBLN_EPS) * g_ref[...] * (1.0 - lam_init))
    o_ref[0] = jnp.where(lane < HEAD_DIM, outs[0], outs[1]).astype(o_ref.dtype)


def _attention_b(proj_b, bias_b, diff_lambda, subln_g2, lam_init):
    batch, seq, _ = proj_b.shape
    pairs = B_HEADS // 2
    n_tiles = seq // BQ
    return pl.pallas_call(
        functools.partial(_attn_b_kernel, lam_init=lam_init),
        grid=(pairs, batch, n_tiles),
        in_specs=[pl.BlockSpec((1, BQ, LANES), lambda hp, b, i: (b, i, hp)),
                  pl.BlockSpec((1, seq, LANES), lambda hp, b, i: (b, 0, pairs + hp)),
                  pl.BlockSpec((1, seq, LANES), lambda hp, b, i: (b, 0, 2 * pairs + hp)),
                  pl.BlockSpec((2, seq // BK, BQ, BK), lambda hp, b, i: (hp, 0, 0, 0),
                               pipeline_mode=pl.Buffered(1)),
                  pl.BlockSpec(diff_lambda.shape, lambda hp, b, i: (0, 0)),
                  pl.BlockSpec(subln_g2.shape, lambda hp, b, i: (0, 0))],
        out_specs=pl.BlockSpec((1, BQ, LANES), lambda hp, b, i: (b, i, hp)),
        out_shape=jax.ShapeDtypeStruct((batch, seq, B_HEADS * HEAD_DIM), BF16),
        scratch_shapes=[pltpu.VMEM((4, BQ, LANES), F32)] * 2,
        compiler_params=pltpu.CompilerParams(dimension_semantics=("arbitrary",) * 3,
                                             vmem_limit_bytes=VMEM_LIMIT),
        name="attention_b",
    )(proj_b, proj_b, proj_b, bias_b, diff_lambda, subln_g2)


def _attn_c_kernel(q_ref, k_ref, v_ref, o_ref, m_ref, acc_ref):
    qi = pl.program_id(2)
    lane = lax.broadcasted_iota(jnp.int32, (BQ, LANES), 1)
    own = _own_head_lanes(BK)
    row = lax.broadcasted_iota(jnp.int32, (BQ, BK), 0)
    colm = lax.broadcasted_iota(jnp.int32, (BQ, BK), 1)

    def step(ki, carry, masked):
        off = pl.multiple_of(ki * BK, BK)
        v = v_ref[0, pl.ds(off, BK), :]
        for j in range(2):
            hs = slice(j * LANES, (j + 1) * LANES)
            s = _nt_dot(q_ref[0, :, hs], k_ref[0, pl.ds(off, BK), hs])
            if masked is not None:
                s = s + jnp.where(row - colm >= masked * BK, 0.0, MASK_VALUE)
            _softmax_step(s, _values_and_ones(v, own[j]), m_ref, acc_ref, j)
        return carry

    _softmax_init(m_ref, acc_ref)
    lax.fori_loop(0, QK_RATIO * qi, functools.partial(step, masked=None), 0)
    for t in range(QK_RATIO):
        step(QK_RATIO * qi + t, 0, t)
    o_ref[0] = jnp.where(lane < HEAD_DIM, _divide_by_row_sum(acc_ref[0]),
                         _divide_by_row_sum(acc_ref[1])).astype(o_ref.dtype)


def _attention_c(qc, kc, vc):
    batch, seq, _ = qc.shape
    pairs = C_HEADS // 2
    n_tiles = seq // BQ
    return pl.pallas_call(
        _attn_c_kernel,
        grid=(pairs, batch, n_tiles),
        in_specs=[pl.BlockSpec((1, BQ, 2 * LANES), lambda hp, b, i: (b, i, hp)),
                  pl.BlockSpec((1, seq, 2 * LANES), lambda hp, b, i: (b, 0, hp)),
                  pl.BlockSpec((1, seq, LANES), lambda hp, b, i: (b, 0, hp))],
        out_specs=pl.BlockSpec((1, BQ, LANES), lambda hp, b, i: (b, i, hp)),
        out_shape=jax.ShapeDtypeStruct((batch, seq, C_HEADS * HEAD_DIM), BF16),
        scratch_shapes=[pltpu.VMEM((2, BQ, LANES), F32)] * 2,
        compiler_params=pltpu.CompilerParams(dimension_semantics=("arbitrary",) * 3,
                                             vmem_limit_bytes=VMEM_LIMIT),
        name="attention_c",
    )(qc, kc, vc)


def _layer_norm(y, g, b):
    mu = jnp.mean(y, axis=-1, keepdims=True)
    yc = y - mu
    var = jnp.mean(yc * yc, axis=-1, keepdims=True)
    return yc * lax.rsqrt(var + LN_EPS) * g + b


def _out_ffn_kernel(x_ref, oa_ref, ob_ref, oc_ref, wo_ref, g1_ref, b1_ref, g2_ref, b2_ref,
                    wg_ref, wu_ref, wd_ref, out_ref, *, alpha):
    heads = jnp.concatenate([oa_ref[...], ob_ref[...], oc_ref[...]], axis=1)
    mix = jnp.dot(heads, wo_ref[...], preferred_element_type=F32)
    h = _layer_norm(alpha * x_ref[...] + mix, g1_ref[...], b1_ref[...])
    hb = h.astype(BF16)
    ffn = jnp.zeros(h.shape, F32)
    for lo, hi in zip(FFN_CHUNK_EDGES[:-1], FFN_CHUNK_EDGES[1:]):
        cs = slice(lo, hi)
        gate = jnp.dot(hb, wg_ref[:, cs], preferred_element_type=F32)
        up = jnp.dot(hb, wu_ref[:, cs], preferred_element_type=F32)
        act = (jax.nn.silu(gate) * up).astype(BF16)
        ffn = ffn + jnp.dot(act, wd_ref[cs, :], preferred_element_type=F32)
    out_ref[...] = _layer_norm(alpha * h + ffn, g2_ref[...], b2_ref[...])


def _out_ffn(x2d, oa, ob, oc, wo, g1, b1, g2, b2, wg, wu, wd, alpha):
    n_tok = x2d.shape[0]
    tm = FFN_TM
    tok = lambda width: pl.BlockSpec((tm, width), lambda i: (i, 0))
    once = lambda a: pl.BlockSpec(a.shape, lambda i: (0, 0), pipeline_mode=pl.Buffered(1))
    return pl.pallas_call(
        functools.partial(_out_ffn_kernel, alpha=alpha),
        grid=(n_tok // tm,),
        in_specs=[tok(D_MODEL), tok(oa.shape[1]), tok(ob.shape[1]), tok(oc.shape[1]),
                  once(wo), once(g1), once(b1), once(g2), once(b2), once(wg), once(wu), once(wd)],
        out_specs=tok(D_MODEL),
        out_shape=jax.ShapeDtypeStruct((n_tok, D_MODEL), F32),
        compiler_params=pltpu.CompilerParams(dimension_semantics=("arbitrary",),
                                             vmem_limit_bytes=VMEM_LIMIT),
        name="out_ffn",
    )(x2d, oa, ob, oc, wo, g1, b1, g2, b2, wg, wu, wd)


def _rotate_half_cols(w):
    half = w.shape[-1] // 2
    return jnp.concatenate([-w[..., half:], w[..., :half]], axis=-1)


def _place_rope(w):
    return jnp.pad(w, ((0, 0), (C_NOPE, LANES - C_NOPE - C_ROPE)))


def _layer_weights(w_in, w_uq, w_ukv):
    main = A_COLS + B_COLS + C_Q_LORA + C_KV_LORA
    k_r = w_in[:, main:]
    w1 = jnp.concatenate([w_in[:, :main], _place_rope(k_r), _place_rope(_rotate_half_cols(k_r))],
                         axis=1).astype(BF16)
    rows = w_uq.shape[0]
    uq = w_uq.reshape(rows, C_HEADS, C_NOPE + C_ROPE)
    nope, rope = uq[..., :C_NOPE], uq[..., C_NOPE:]
    tail = jnp.zeros((rows, C_HEADS, LANES - C_NOPE - C_ROPE), w_uq.dtype)
    q_main = jnp.concatenate([nope, rope, tail], axis=-1).reshape(rows, C_HEADS * LANES)
    q_half = jnp.concatenate([jnp.zeros_like(nope), _rotate_half_cols(rope), tail],
                             axis=-1).reshape(rows, C_HEADS * LANES)
    wq = jnp.concatenate([q_main, q_half], axis=1).astype(BF16)
    rows = w_ukv.shape[0]
    ukv = w_ukv.reshape(rows, C_HEADS, C_NOPE + HEAD_DIM)
    k_nope, v = ukv[..., :C_NOPE], ukv[..., C_NOPE:]
    k_main = jnp.concatenate([k_nope, jnp.zeros((rows, C_HEADS, LANES - C_NOPE), w_ukv.dtype)],
                             axis=-1).reshape(rows, C_HEADS * LANES)
    wkv = jnp.concatenate([k_main, v.reshape(rows, C_HEADS * HEAD_DIM)], axis=1).astype(BF16)
    return w1, wq, wkv


def _rope_tables(seq):
    half = C_ROPE // 2
    inv = ROPE_THETA ** (-np.arange(half, dtype=np.float64) / half)
    ang = np.arange(seq, dtype=np.float64)[:, None] * inv[None, :]
    cos = np.zeros((seq, LANES), np.float64)
    sin = np.zeros((seq, LANES), np.float64)
    cos[:, :C_NOPE] = 1.0
    cos[:, C_NOPE:C_NOPE + C_ROPE] = np.concatenate([np.cos(ang), np.cos(ang)], axis=1)
    sin[:, C_NOPE:C_NOPE + C_ROPE] = np.concatenate([np.sin(ang), np.sin(ang)], axis=1)
    return jnp.asarray(cos, F32), jnp.asarray(sin, F32)


def kernel(x, rel_bias, w_in, q_norm_g, kv_norm_g, w_uq, w_ukv, diff_lambda, subln_g, w_o, ln1_g, ln1_b, ln2_g, ln2_b, w_gate, w_up, w_down):
    batch, seq, d_model = x.shape
    depth = w_in.shape[0]
    alpha = (2 * depth) ** 0.25
    bias_a = _bias_tiles(rel_bias, A_HEADS, 0, len(A_PAIRS), A_BLK, 2 * A_BLK, True)
    bias_b = _bias_tiles(rel_bias, B_HEADS, A_HEADS, seq // BK, BQ, BK, False, QK_RATIO - 1)
    cos_t, sin_t = _rope_tables(seq)
    row = lambda a: a.reshape(1, -1)
    x2d = x.reshape(batch * seq, d_model)
    for l in range(depth):
        w1, wq, wkv = _layer_weights(w_in[l], w_uq[l], w_ukv[l])
        pa, pb, qc, kc, vc = _projection(x2d, w1, row(q_norm_g[l]), row(kv_norm_g[l]), wq, wkv,
                                         cos_t, sin_t, seq)
        lam_init = 0.8 - 0.6 * math.exp(-0.3 * l)
        oa = _attention_a(pa.reshape(batch, seq, -1), bias_a)
        ob = _attention_b(pb.reshape(batch, seq, -1), bias_b, diff_lambda[l],
                          row(jnp.concatenate([subln_g[l], subln_g[l]])), lam_init)
        oc = _attention_c(qc.reshape(batch, seq, -1), kc.reshape(batch, seq, -1),
                          vc.reshape(batch, seq, -1))
        flat = lambda a: a.reshape(batch * seq, -1)
        x2d = _out_ffn(x2d, flat(oa), flat(ob), flat(oc), w_o[l].astype(BF16),
                       row(ln1_g[l]), row(ln1_b[l]), row(ln2_g[l]), row(ln2_b[l]),
                       w_gate[l].astype(BF16), w_up[l].astype(BF16), w_down[l].astype(BF16), alpha)
    return x2d.reshape(batch, seq, d_model)
```

```python
import functools
import math

import numpy as np
import jax
import jax.numpy as jnp
from jax import lax
from jax.experimental import pallas as pl
from jax.experimental.pallas import tpu as pltpu

F32 = jnp.float32
BF16 = jnp.bfloat16

D_MODEL = 1024
HEAD_DIM = 64
A_HEADS = 6
A_PAIRS = ((128, 1), (512, 4), (2048, 16))
A_BLK = 128
B_HEADS = 4
B_QK_DIM = 32
C_HEADS = 6
C_Q_LORA = 256
C_KV_LORA = 128
C_NOPE = 64
C_ROPE = 32
ROPE_THETA = 10000.0
A_COLS = 3 * A_HEADS * HEAD_DIM
B_COLS = B_HEADS * 3 * HEAD_DIM
NUM_BUCKETS = 32
MAX_DISTANCE = 2048
FF_DIM = 2816
LN_EPS = 1e-5
LATENT_EPS = 1e-6
SUBLN_EPS = 1e-5

LANES = 128
MASK_VALUE = -1e30
LOG2E = math.log2(math.e)
A_QSCALE = HEAD_DIM ** -0.5 * LOG2E
B_QSCALE = B_QK_DIM ** -0.5 * LOG2E
C_QSCALE = (C_NOPE + C_ROPE) ** -0.5 * LOG2E
VMEM_LIMIT = 56 * 1024 * 1024

PROJ_TM = 512
FFN_TM = 512
MXU_WIDTH = 256
FFN_CHUNK_EDGES = (0, 5 * MXU_WIDTH, FF_DIM)
BQ = 512
BK = BQ
HALF = BK // 2


def _bucket_lower_bounds():
    max_exact = NUM_BUCKETS // 2
    d = np.arange(0, MAX_DISTANCE + 1)
    val = np.log(np.maximum(d, 1) / max_exact) / math.log(MAX_DISTANCE / max_exact) * (NUM_BUCKETS - max_exact)
    large = np.minimum(max_exact + np.floor(np.maximum(val, 0.0)).astype(np.int64), NUM_BUCKETS - 1)
    bucket = np.where(d < max_exact, d, large)
    return [int(np.argmax(bucket >= b)) for b in range(NUM_BUCKETS)]


_BUCKET_LO = _bucket_lower_bounds()


def _nt_dot(a, b):
    return lax.dot_general(a, b, (((1,), (1,)), ((), ())), preferred_element_type=F32)


def _own_head_lanes(rows):
    lane = lax.broadcasted_iota(jnp.int32, (rows, LANES), 1)
    return lane < HEAD_DIM, lane >= HEAD_DIM


def _values_and_ones(v, own):
    return jnp.where(own, v, jnp.ones_like(v))


def _divide_by_row_sum(acc):
    return acc / pltpu.roll(acc, HEAD_DIM, 1)


def _bias_tile_kernel(rb_ref, out_ref, *, head_offset, dilated):
    h = pl.program_id(0) + head_offset
    g = pl.program_id(1)
    rows, cols = out_ref.shape[2], out_ref.shape[3]
    i = lax.broadcasted_iota(jnp.int32, (rows, cols), 0)
    j = lax.broadcasted_iota(jnp.int32, (rows, cols), 1)
    if dilated:
        rel = i + A_BLK - j
        valid = (rel >= 0) & (rel <= A_BLK)
        dil = jnp.where(g == 0, A_PAIRS[0][1], jnp.where(g == 1, A_PAIRS[1][1], A_PAIRS[2][1]))
        dist = rel * dil
    else:
        dist = g * cols + i - j
        valid = dist >= 0
    val = jnp.full((rows, cols), rb_ref[0, h], F32)
    for b in range(1, NUM_BUCKETS):
        val = jnp.where(dist >= _BUCKET_LO[b], rb_ref[b, h], val)
    out_ref[0, 0] = jnp.where(valid, val * LOG2E, MASK_VALUE)


def _bias_tiles(rel_bias, n_heads, head_offset, n_groups, rows, cols, dilated):
    return pl.pallas_call(
        functools.partial(_bias_tile_kernel, head_offset=head_offset, dilated=dilated),
        grid=(n_heads, n_groups),
        in_specs=[pl.BlockSpec(memory_space=pltpu.SMEM)],
        out_specs=pl.BlockSpec((1, 1, rows, cols), lambda h, g: (h, g, 0, 0)),
        out_shape=jax.ShapeDtypeStruct((n_heads, n_groups, rows, cols), F32),
        name="bias_tiles",
    )(rel_bias)


def _proj_kernel(x_ref, w1_ref, qg_ref, kvg_ref, wq_ref, wkv_ref, cos_ref, sin_ref,
                 pa_ref, pb_ref, qc_ref, kc_ref, vc_ref):
    xb = x_ref[...].astype(BF16)
    proj = jnp.dot(xb, w1_ref[...], preferred_element_type=F32)
    a_q = A_HEADS * HEAD_DIM
    pa_ref[:, :a_q] = proj[:, :a_q] * A_QSCALE
    pa_ref[:, a_q:] = proj[:, a_q:A_COLS]
    b_q = B_HEADS * HEAD_DIM
    pb_ref[:, :b_q] = (proj[:, A_COLS:A_COLS + b_q] * B_QSCALE).astype(BF16)
    pb_ref[:, b_q:] = proj[:, A_COLS + b_q:A_COLS + B_COLS].astype(BF16)
    c = proj[:, A_COLS + B_COLS:]
    c_q = c[:, :C_Q_LORA]
    c_kv = c[:, C_Q_LORA:C_Q_LORA + C_KV_LORA]
    k_rot = c[:, C_Q_LORA + C_KV_LORA:C_Q_LORA + C_KV_LORA + LANES]
    k_rot_half = c[:, C_Q_LORA + C_KV_LORA + LANES:]
    cqn = (c_q * lax.rsqrt(jnp.mean(c_q * c_q, axis=-1, keepdims=True) + LATENT_EPS) * qg_ref[...]).astype(BF16)
    ckvn = (c_kv * lax.rsqrt(jnp.mean(c_kv * c_kv, axis=-1, keepdims=True) + LATENT_EPS) * kvg_ref[...]).astype(BF16)
    q2 = jnp.dot(cqn, wq_ref[...], preferred_element_type=F32)
    kv = jnp.dot(ckvn, wkv_ref[...], preferred_element_type=F32)
    cos = cos_ref[...]
    sin = sin_ref[...]
    k_rope = k_rot * cos + k_rot_half * sin
    qw = C_HEADS * LANES
    for h in range(C_HEADS):
        sl = slice(h * LANES, (h + 1) * LANES)
        sl2 = slice(qw + h * LANES, qw + (h + 1) * LANES)
        qc_ref[:, sl] = ((q2[:, sl] * cos + q2[:, sl2] * sin) * C_QSCALE).astype(BF16)
        kc_ref[:, sl] = (kv[:, sl] + k_rope).astype(BF16)
    vc_ref[...] = kv[:, qw:].astype(BF16)


def _projection(x2d, w1, qg, kvg, wq, wkv, cos_t, sin_t, seq):
    n_tok = x2d.shape[0]
    tm = PROJ_TM
    pos_blocks = seq // tm
    full = lambda shape: pl.BlockSpec(shape, lambda i: (0, 0))
    tok = lambda width: pl.BlockSpec((tm, width), lambda i: (i, 0))
    pos = pl.BlockSpec((tm, LANES), lambda i: (i % pos_blocks, 0))
    return pl.pallas_call(
        _proj_kernel,
        grid=(n_tok // tm,),
        in_specs=[tok(D_MODEL), full(w1.shape), full(qg.shape), full(kvg.shape), full(wq.shape),
                  full(wkv.shape), pos, pos],
        out_specs=[tok(A_COLS), tok(B_COLS), tok(C_HEADS * LANES), tok(C_HEADS * LANES),
                   tok(C_HEADS * HEAD_DIM)],
        out_shape=[jax.ShapeDtypeStruct((n_tok, A_COLS), F32),
                   jax.ShapeDtypeStruct((n_tok, B_COLS), BF16),
                   jax.ShapeDtypeStruct((n_tok, C_HEADS * LANES), BF16),
                   jax.ShapeDtypeStruct((n_tok, C_HEADS * LANES), BF16),
                   jax.ShapeDtypeStruct((n_tok, C_HEADS * HEAD_DIM), BF16)],
        compiler_params=pltpu.CompilerParams(dimension_semantics=("arbitrary",),
                                             vmem_limit_bytes=VMEM_LIMIT),
        name="projection",
    )(x2d, w1, qg, kvg, wq, wkv, cos_t, sin_t)


def _attn_a_kernel(q_ref, k_ref, v_ref, bias_ref, o_ref, d4_ref, d16_ref, r0_ref, r1c_ref, r1_ref,
                   r2c_ref, r2m_ref, r2_ref):
    seq = q_ref.shape[1]
    len4, len16 = seq // 4, seq // 16
    lane = lax.broadcasted_iota(jnp.int32, (A_BLK, LANES), 1)
    lo_half = lane < HEAD_DIM
    ones = jnp.ones((2 * A_BLK, LANES), BF16)

    for a, src in enumerate((q_ref, k_ref, v_ref)):
        for r4 in range(4):
            d4_ref[a, r4] = src[0, pl.ds(r4, len4, stride=4), :]
        for r4 in range(4):
            for hi in range(4):
                d16_ref[a, 4 * hi + r4] = d4_ref[a, r4, pl.ds(hi, len16, stride=4), :]

    def attend(qf, kf, vf, p, first):
        nk = kf.shape[0]
        kb = kf.astype(BF16)
        v_aug = jnp.concatenate([vf.astype(BF16), ones[:nk]], axis=1)
        out, lse = [], []
        for j in range(2):
            qj = jnp.where(lo_half if j == 0 else jnp.logical_not(lo_half), qf, 0.0).astype(BF16)
            bias = bias_ref[j, p]
            if first:
                bias = bias[:, A_BLK:]
            s = _nt_dot(qj, kb) + bias
            m = jnp.max(s, axis=-1, keepdims=True)
            pr = jnp.exp2(s - m).astype(BF16)
            r = jnp.dot(pr, v_aug, preferred_element_type=F32)
            den = r[:, LANES:]
            out.append(r[:, :LANES] / den)
            lse.append(m + jnp.log2(den))
        return jnp.where(lo_half, out[0], out[1]), jnp.where(lo_half, lse[0], lse[1])

    def key_rows(start, first):
        return pl.ds(start, A_BLK) if first else pl.ds(start - A_BLK, 2 * A_BLK)

    def pair0_block(start, first):
        kr = key_rows(start, first)
        out, lse = attend(q_ref[0, pl.ds(start, A_BLK), :], k_ref[0, kr, :], v_ref[0, kr, :], 0, first)
        r0_ref[0, pl.ds(start, A_BLK), :] = out
        r0_ref[1, pl.ds(start, A_BLK), :] = lse

    for n in range(seq // A_BLK):
        pair0_block(n * A_BLK, n == 0)

    def pair1_class(r4, c):
        for n in range(len4 // A_BLK):
            kr = key_rows(n * A_BLK, n == 0)
            out, lse = attend(d4_ref[0, r4, pl.ds(n * A_BLK, A_BLK), :], d4_ref[1, r4, kr, :],
                              d4_ref[2, r4, kr, :], 1, n == 0)
            r1c_ref[0, r4, pl.ds(n * A_BLK, A_BLK), :] = out
            r1c_ref[1, r4, pl.ds(n * A_BLK, A_BLK), :] = lse
        return c
    for r4 in range(4):
        pair1_class(r4, 0)

    def pair2_class(r, c):
        out, lse = attend(d16_ref[0, r], d16_ref[1, r], d16_ref[2, r], 2, True)
        r2c_ref[0, r] = out
        r2c_ref[1, r] = lse
        return c
    for r in range(16):
        pair2_class(r, 0)

    for c in range(2):
        for r4 in range(4):
            for hi in range(4):
                r2m_ref[c, r4, pl.ds(hi, len16, stride=4), :] = r2c_ref[c, 4 * hi + r4]
        for r4 in range(4):
            r1_ref[c, pl.ds(r4, len4, stride=4), :] = r1c_ref[c, r4]
            r2_ref[c, pl.ds(r4, len4, stride=4), :] = r2m_ref[c, r4]

    chunk = 256

    def combine(i, carry):
        rs = pl.ds(pl.multiple_of(i * chunk, chunk), chunk)
        l0, l1, l2 = r0_ref[1, rs, :], r1_ref[1, rs, :], r2_ref[1, rs, :]
        mx = jnp.maximum(jnp.maximum(l0, l1), l2)
        e0, e1, e2 = jnp.exp2(l0 - mx), jnp.exp2(l1 - mx), jnp.exp2(l2 - mx)
        num = e0 * r0_ref[0, rs, :] + e1 * r1_ref[0, rs, :] + e2 * r2_ref[0, rs, :]
        o_ref[0, rs, :] = (num / (e0 + e1 + e2)).astype(o_ref.dtype)
        return carry

    lax.fori_loop(0, seq // chunk, combine, 0)


def _attention_a(proj_a, bias_a):
    batch, seq, _ = proj_a.shape
    pairs = A_HEADS // 2
    n_pairs = len(A_PAIRS)
    col = lambda off: pl.BlockSpec((1, seq, LANES), lambda hp, b: (b, 0, off + hp))
    result = pltpu.VMEM((2, seq, LANES), F32)
    by4 = pltpu.VMEM((2, 4, seq // 4, LANES), F32)
    return pl.pallas_call(
        _attn_a_kernel,
        grid=(pairs, batch),
        in_specs=[col(0), col(pairs), col(2 * pairs),
                  pl.BlockSpec((2, n_pairs, A_BLK, 2 * A_BLK), lambda hp, b: (hp, 0, 0, 0))],
        out_specs=pl.BlockSpec((1, seq, LANES), lambda hp, b: (b, 0, hp)),
        out_shape=jax.ShapeDtypeStruct((batch, seq, A_HEADS * HEAD_DIM), BF16),
        scratch_shapes=[pltpu.VMEM((3, 4, seq // 4, LANES), F32), pltpu.VMEM((3, 16, seq // 16, LANES), F32),
                        result, by4, result, pltpu.VMEM((2, 16, seq // 16, LANES), F32), by4, result],
        compiler_params=pltpu.CompilerParams(dimension_semantics=("arbitrary", "arbitrary"),
                                             vmem_limit_bytes=VMEM_LIMIT),
        name="attention_a",
    )(proj_a, proj_a, proj_a, bias_a)


def _softmax_init(m_ref, acc_ref):
    m_ref[...] = jnp.full(m_ref.shape, MASK_VALUE, F32)
    acc_ref[...] = jnp.zeros(acc_ref.shape, F32)


def _softmax_step(s, v_aug, m_ref, acc_ref, c):
    rows = slice(BQ - s.shape[0], BQ)
    m_old = m_ref[c, rows]
    m_new = jnp.max(jnp.concatenate([s, m_old], axis=1), axis=-1, keepdims=True)
    alpha = jnp.exp2(m_old - m_new)
    pr = jnp.exp2(s - m_new).astype(BF16)
    acc_ref[c, rows] = acc_ref[c, rows] * alpha + jnp.dot(pr, v_aug, preferred_element_type=F32)
    m_ref[c, rows] = jnp.broadcast_to(m_new, m_old.shape)


def _attn_b_kernel(q_ref, k_ref, v_ref, bias_ref, lam_ref, g_ref, o_ref, m_ref, acc_ref, *, lam_init):
    qi = pl.program_id(2)
    q = q_ref[0]
    lane = lax.broadcasted_iota(jnp.int32, (BQ, LANES), 1)
    own = _own_head_lanes(BK)
    lf = lam_ref[...]
    dot_rows = lambda a, b: jnp.sum(lf[a:a + 1] * lf[b:b + 1], axis=-1, keepdims=True)
    lam = jnp.exp(dot_rows(0, 1)) - jnp.exp(dot_rows(2, 3)) + lam_init

    qs = jnp.concatenate(
        [jnp.where((lane >= c * B_QK_DIM) & (lane < (c + 1) * B_QK_DIM), q, jnp.zeros_like(q))
         for c in range(4)], axis=0)

    def step(ki, carry):
        off = pl.multiple_of(ki * BK, BK)
        k = k_ref[0, pl.ds(off, BK), :]
        v = v_ref[0, pl.ds(off, BK), :]
        v_aug = [_values_and_ones(v, own[j]) for j in range(2)]
        s_all = _nt_dot(qs, k)
        for c in range(4):
            bias = bias_ref[c // 2, qi - ki]
            _softmax_step(s_all[c * BQ:(c + 1) * BQ] + bias, v_aug[c // 2], m_ref, acc_ref, c)
        return carry

    _softmax_init(m_ref, acc_ref)
    lax.fori_loop(0, qi + 1, step, 0)

    outs = []
    for j in range(2):
        base = j * HEAD_DIM
        a1, a2 = acc_ref[2 * j], acc_ref[2 * j + 1]
        o = _divide_by_row_sum(a1) - lam * _divide_by_row_sum(a2)
        in_head = (lane >= base) & (lane < base + HEAD_DIM)
        ms = jnp.sum(jnp.where(in_head, o * o, 0.0), axis=-1, keepdims=True) * (1.0 / HEAD_DIM)
        outs.append(o * lax.rsqrt(ms + SUBLN_EPS) * g_ref[...] * (1.0 - lam_init))
    o_ref[0] = jnp.where(lane < HEAD_DIM, outs[0], outs[1]).astype(o_ref.dtype)


def _attention_b(proj_b, bias_b, diff_lambda, subln_g2, lam_init):
    batch, seq, _ = proj_b.shape
    pairs = B_HEADS // 2
    n_tiles = seq // BQ
    return pl.pallas_call(
        functools.partial(_attn_b_kernel, lam_init=lam_init),
        grid=(pairs, batch, n_tiles),
        in_specs=[pl.BlockSpec((1, BQ, LANES), lambda hp, b, i: (b, i, hp)),
                  pl.BlockSpec((1, seq, LANES), lambda hp, b, i: (b, 0, pairs + hp)),
                  pl.BlockSpec((1, seq, LANES), lambda hp, b, i: (b, 0, 2 * pairs + hp)),
                  pl.BlockSpec((2, seq // BK, BQ, BK), lambda hp, b, i: (hp, 0, 0, 0),
                               pipeline_mode=pl.Buffered(1)),
                  pl.BlockSpec(diff_lambda.shape, lambda hp, b, i: (0, 0)),
                  pl.BlockSpec(subln_g2.shape, lambda hp, b, i: (0, 0))],
        out_specs=pl.BlockSpec((1, BQ, LANES), lambda hp, b, i: (b, i, hp)),
        out_shape=jax.ShapeDtypeStruct((batch, seq, B_HEADS * HEAD_DIM), BF16),
        scratch_shapes=[pltpu.VMEM((4, BQ, LANES), F32)] * 2,
        compiler_params=pltpu.CompilerParams(dimension_semantics=("arbitrary",) * 3,
                                             vmem_limit_bytes=VMEM_LIMIT),
        name="attention_b",
    )(proj_b, proj_b, proj_b, bias_b, diff_lambda, subln_g2)


def _attn_c_kernel(q_ref, k_ref, v_ref, o_ref, m_ref, acc_ref):
    qi = pl.program_id(2)
    lane = lax.broadcasted_iota(jnp.int32, (BQ, LANES), 1)
    own = _own_head_lanes(BK)

    def step(ki, carry):
        off = pl.multiple_of(ki * BK, BK)
        v = v_ref[0, pl.ds(off, BK), :]
        for j in range(2):
            hs = slice(j * LANES, (j + 1) * LANES)
            s = _nt_dot(q_ref[0, :, hs], k_ref[0, pl.ds(off, BK), hs])
            _softmax_step(s, _values_and_ones(v, own[j]), m_ref, acc_ref, j)
        return carry

    _softmax_init(m_ref, acc_ref)
    lax.fori_loop(0, qi, step, 0)

    diag = pl.multiple_of(qi * BK, BK)
    own_half = _own_head_lanes(HALF)
    for q_rows, k_lo in ((BQ, 0), (HALF, HALF)):
        row = lax.broadcasted_iota(jnp.int32, (q_rows, HALF), 0)
        colm = lax.broadcasted_iota(jnp.int32, (q_rows, HALF), 1)
        causal = jnp.where(row >= colm, 0.0, MASK_VALUE)
        v = v_ref[0, pl.ds(diag + k_lo, HALF), :]
        for j in range(2):
            hs = slice(j * LANES, (j + 1) * LANES)
            s = _nt_dot(q_ref[0, BQ - q_rows:, hs], k_ref[0, pl.ds(diag + k_lo, HALF), hs]) + causal
            _softmax_step(s, _values_and_ones(v, own_half[j]), m_ref, acc_ref, j)
    o_ref[0] = jnp.where(lane < HEAD_DIM, _divide_by_row_sum(acc_ref[0]),
                         _divide_by_row_sum(acc_ref[1])).astype(o_ref.dtype)


def _attention_c(qc, kc, vc):
    batch, seq, _ = qc.shape
    pairs = C_HEADS // 2
    n_tiles = seq // BQ
    return pl.pallas_call(
        _attn_c_kernel,
        grid=(pairs, batch, n_tiles),
        in_specs=[pl.BlockSpec((1, BQ, 2 * LANES), lambda hp, b, i: (b, i, hp)),
                  pl.BlockSpec((1, seq, 2 * LANES), lambda hp, b, i: (b, 0, hp)),
                  pl.BlockSpec((1, seq, LANES), lambda hp, b, i: (b, 0, hp))],
        out_specs=pl.BlockSpec((1, BQ, LANES), lambda hp, b, i: (b, i, hp)),
        out_shape=jax.ShapeDtypeStruct((batch, seq, C_HEADS * HEAD_DIM), BF16),
        scratch_shapes=[pltpu.VMEM((2, BQ, LANES), F32)] * 2,
        compiler_params=pltpu.CompilerParams(dimension_semantics=("arbitrary",) * 3,
                                             vmem_limit_bytes=VMEM_LIMIT),
        name="attention_c",
    )(qc, kc, vc)


def _layer_norm(y, g, b):
    mu = jnp.mean(y, axis=-1, keepdims=True)
    yc = y - mu
    var = jnp.mean(yc * yc, axis=-1, keepdims=True)
    return yc * lax.rsqrt(var + LN_EPS) * g + b


def _out_ffn_kernel(x_ref, oa_ref, ob_ref, oc_ref, wo_ref, g1_ref, b1_ref, g2_ref, b2_ref,
                    wg_ref, wu_ref, wd_ref, out_ref, *, alpha):
    heads = jnp.concatenate([oa_ref[...], ob_ref[...], oc_ref[...]], axis=1)
    mix = jnp.dot(heads, wo_ref[...], preferred_element_type=F32)
    h = _layer_norm(alpha * x_ref[...] + mix, g1_ref[...], b1_ref[...])
    hb = h.astype(BF16)
    ffn = jnp.zeros(h.shape, F32)
    for lo, hi in zip(FFN_CHUNK_EDGES[:-1], FFN_CHUNK_EDGES[1:]):
        cs = slice(lo, hi)
        gate = jnp.dot(hb, wg_ref[:, cs], preferred_element_type=F32)
        up = jnp.dot(hb, wu_ref[:, cs], preferred_element_type=F32)
        act = (jax.nn.silu(gate) * up).astype(BF16)
        ffn = ffn + jnp.dot(act, wd_ref[cs, :], preferred_element_type=F32)
    out_ref[...] = _layer_norm(alpha * h + ffn, g2_ref[...], b2_ref[...])


def _out_ffn(x2d, oa, ob, oc, wo, g1, b1, g2, b2, wg, wu, wd, alpha):
    n_tok = x2d.shape[0]
    tm = FFN_TM
    tok = lambda width: pl.BlockSpec((tm, width), lambda i: (i, 0))
    once = lambda a: pl.BlockSpec(a.shape, lambda i: (0, 0), pipeline_mode=pl.Buffered(1))
    return pl.pallas_call(
        functools.partial(_out_ffn_kernel, alpha=alpha),
        grid=(n_tok // tm,),
        in_specs=[tok(D_MODEL), tok(oa.shape[1]), tok(ob.shape[1]), tok(oc.shape[1]),
                  once(wo), once(g1), once(b1), once(g2), once(b2), once(wg), once(wu), once(wd)],
        out_specs=tok(D_MODEL),
        out_shape=jax.ShapeDtypeStruct((n_tok, D_MODEL), F32),
        compiler_params=pltpu.CompilerParams(dimension_semantics=("arbitrary",),
                                             vmem_limit_bytes=VMEM_LIMIT),
        name="out_ffn",
    )(x2d, oa, ob, oc, wo, g1, b1, g2, b2, wg, wu, wd)


def _rotate_half_cols(w):
    half = w.shape[-1] // 2
    return jnp.concatenate([-w[..., half:], w[..., :half]], axis=-1)


def _place_rope(w):
    return jnp.pad(w, ((0, 0), (C_NOPE, LANES - C_NOPE - C_ROPE)))


def _layer_weights(w_in, w_uq, w_ukv):
    main = A_COLS + B_COLS + C_Q_LORA + C_KV_LORA
    k_r = w_in[:, main:]
    w1 = jnp.concatenate([w_in[:, :main], _place_rope(k_r), _place_rope(_rotate_half_cols(k_r))],
                         axis=1).astype(BF16)
    rows = w_uq.shape[0]
    uq = w_uq.reshape(rows, C_HEADS, C_NOPE + C_ROPE)
    nope, rope = uq[..., :C_NOPE], uq[..., C_NOPE:]
    tail = jnp.zeros((rows, C_HEADS, LANES - C_NOPE - C_ROPE), w_uq.dtype)
    q_main = jnp.concatenate([nope, rope, tail], axis=-1).reshape(rows, C_HEADS * LANES)
    q_half = jnp.concatenate([jnp.zeros_like(nope), _rotate_half_cols(rope), tail],
                             axis=-1).reshape(rows, C_HEADS * LANES)
    wq = jnp.concatenate([q_main, q_half], axis=1).astype(BF16)
    rows = w_ukv.shape[0]
    ukv = w_ukv.reshape(rows, C_HEADS, C_NOPE + HEAD_DIM)
    k_nope, v = ukv[..., :C_NOPE], ukv[..., C_NOPE:]
    k_main = jnp.concatenate([k_nope, jnp.zeros((rows, C_HEADS, LANES - C_NOPE), w_ukv.dtype)],
                             axis=-1).reshape(rows, C_HEADS * LANES)
    wkv = jnp.concatenate([k_main, v.reshape(rows, C_HEADS * HEAD_DIM)], axis=1).astype(BF16)
    return w1, wq, wkv


def _rope_tables(seq):
    half = C_ROPE // 2
    inv = ROPE_THETA ** (-np.arange(half, dtype=np.float64) / half)
    ang = np.arange(seq, dtype=np.float64)[:, None] * inv[None, :]
    cos = np.zeros((seq, LANES), np.float64)
    sin = np.zeros((seq, LANES), np.float64)
    cos[:, :C_NOPE] = 1.0
    cos[:, C_NOPE:C_NOPE + C_ROPE] = np.concatenate([np.cos(ang), np.cos(ang)], axis=1)
    sin[:, C_NOPE:C_NOPE + C_ROPE] = np.concatenate([np.sin(ang), np.sin(ang)], axis=1)
    return jnp.asarray(cos, F32), jnp.asarray(sin, F32)


def kernel(x, rel_bias, w_in, q_norm_g, kv_norm_g, w_uq, w_ukv, diff_lambda, subln_g, w_o, ln1_g, ln1_b, ln2_g, ln2_b, w_gate, w_up, w_down):
    batch, seq, d_model = x.shape
    depth = w_in.shape[0]
    alpha = (2 * depth) ** 0.25
    bias_a = _bias_tiles(rel_bias, A_HEADS, 0, len(A_PAIRS), A_BLK, 2 * A_BLK, True)
    bias_b = _bias_tiles(rel_bias, B_HEADS, A_HEADS, seq // BK, BQ, BK, False)
    cos_t, sin_t = _rope_tables(seq)
    row = lambda a: a.reshape(1, -1)
    x2d = x.reshape(batch * seq, d_model)
    for l in range(depth):
        w1, wq, wkv = _layer_weights(w_in[l], w_uq[l], w_ukv[l])
        pa, pb, qc, kc, vc = _projection(x2d, w1, row(q_norm_g[l]), row(kv_norm_g[l]), wq, wkv,
                                         cos_t, sin_t, seq)
        lam_init = 0.8 - 0.6 * math.exp(-0.3 * l)
        oa = _attention_a(pa.reshape(batch, seq, -1), bias_a)
        ob = _attention_b(pb.reshape(batch, seq, -1), bias_b, diff_lambda[l],
                          row(jnp.concatenate([subln_g[l], subln_g[l]])), lam_init)
        oc = _attention_c(qc.reshape(batch, seq, -1), kc.reshape(batch, seq, -1),
                          vc.reshape(batch, seq, -1))
        flat = lambda a: a.reshape(batch * seq, -1)
        x2d = _out_ffn(x2d, flat(oa), flat(ob), flat(oc), w_o[l].astype(BF16),
                       row(ln1_g[l]), row(ln1_b[l]), row(ln2_g[l]), row(ln2_b[l]),
                       w_gate[l].astype(BF16), w_up[l].astype(BF16), w_down[l].astype(BF16), alpha)
    return x2d.reshape(batch, seq, d_model)
```

```python
import functools
import math

import numpy as np
import jax
import jax.numpy as jnp
from jax import lax
from jax.experimental import pallas as pl
from jax.experimental.pallas import tpu as pltpu

F32 = jnp.float32
BF16 = jnp.bfloat16

D_MODEL = 1024
HEAD_DIM = 64
A_HEADS = 6
A_PAIRS = ((128, 1), (512, 4), (2048, 16))
A_BLK = 128
B_HEADS = 4
B_QK_DIM = 32
C_HEADS = 6
C_Q_LORA = 256
C_KV_LORA = 128
C_NOPE = 64
C_ROPE = 32
ROPE_THETA = 10000.0
A_COLS = 3 * A_HEADS * HEAD_DIM
B_COLS = B_HEADS * 3 * HEAD_DIM
NUM_BUCKETS = 32
MAX_DISTANCE = 2048
FF_DIM = 2816
LN_EPS = 1e-5
LATENT_EPS = 1e-6
SUBLN_EPS = 1e-5

LANES = 128
MASK_VALUE = -1e30
LOG2E = math.log2(math.e)
A_QSCALE = HEAD_DIM ** -0.5 * LOG2E
B_QSCALE = B_QK_DIM ** -0.5 * LOG2E
C_QSCALE = (C_NOPE + C_ROPE) ** -0.5 * LOG2E
VMEM_LIMIT = 56 * 1024 * 1024

PROJ_TM = 512
FFN_TM = 512
MXU_WIDTH = 256
FFN_CHUNK_EDGES = (0, 5 * MXU_WIDTH, FF_DIM)
BQ = 512
BK = BQ
HALF = BK // 2


def _bucket_lower_bounds():
    max_exact = NUM_BUCKETS // 2
    d = np.arange(0, MAX_DISTANCE + 1)
    val = np.log(np.maximum(d, 1) / max_exact) / math.log(MAX_DISTANCE / max_exact) * (NUM_BUCKETS - max_exact)
    large = np.minimum(max_exact + np.floor(np.maximum(val, 0.0)).astype(np.int64), NUM_BUCKETS - 1)
    bucket = np.where(d < max_exact, d, large)
    return [int(np.argmax(bucket >= b)) for b in range(NUM_BUCKETS)]


_BUCKET_LO = _bucket_lower_bounds()


def _nt_dot(a, b):
    return lax.dot_general(a, b, (((1,), (1,)), ((), ())), preferred_element_type=F32)


def _own_head_lanes(rows):
    lane = lax.broadcasted_iota(jnp.int32, (rows, LANES), 1)
    return lane < HEAD_DIM, lane >= HEAD_DIM


def _values_and_ones(v, own):
    return jnp.where(own, v, jnp.ones_like(v))


def _divide_by_row_sum(acc):
    return acc / pltpu.roll(acc, HEAD_DIM, 1)


def _bias_tile_kernel(rb_ref, out_ref, *, head_offset, dilated):
    h = pl.program_id(0) + head_offset
    g = pl.program_id(1)
    rows, cols = out_ref.shape[2], out_ref.shape[3]
    i = lax.broadcasted_iota(jnp.int32, (rows, cols), 0)
    j = lax.broadcasted_iota(jnp.int32, (rows, cols), 1)
    if dilated:
        rel = i + A_BLK - j
        valid = (rel >= 0) & (rel <= A_BLK)
        dil = jnp.where(g == 0, A_PAIRS[0][1], jnp.where(g == 1, A_PAIRS[1][1], A_PAIRS[2][1]))
        dist = rel * dil
    else:
        dist = g * cols + i - j
        valid = dist >= 0
    val = jnp.full((rows, cols), rb_ref[0, h], F32)
    for b in range(1, NUM_BUCKETS):
        val = jnp.where(dist >= _BUCKET_LO[b], rb_ref[b, h], val)
    out_ref[0, 0] = jnp.where(valid, val * LOG2E, MASK_VALUE)


def _bias_tiles(rel_bias, n_heads, head_offset, n_groups, rows, cols, dilated):
    return pl.pallas_call(
        functools.partial(_bias_tile_kernel, head_offset=head_offset, dilated=dilated),
        grid=(n_heads, n_groups),
        in_specs=[pl.BlockSpec(memory_space=pltpu.SMEM)],
        out_specs=pl.BlockSpec((1, 1, rows, cols), lambda h, g: (h, g, 0, 0)),
        out_shape=jax.ShapeDtypeStruct((n_heads, n_groups, rows, cols), F32),
        name="bias_tiles",
    )(rel_bias)


def _proj_kernel(x_ref, w1_ref, qg_ref, kvg_ref, wq_ref, wkv_ref, cos_ref, sin_ref,
                 pa_ref, pb_ref, qc_ref, kc_ref, vc_ref):
    xb = x_ref[...].astype(BF16)
    proj = jnp.dot(xb, w1_ref[...], preferred_element_type=F32)
    a_q = A_HEADS * HEAD_DIM
    pa_ref[:, :a_q] = proj[:, :a_q] * A_QSCALE
    pa_ref[:, a_q:] = proj[:, a_q:A_COLS]
    b_q = B_HEADS * HEAD_DIM
    pb_ref[:, :b_q] = (proj[:, A_COLS:A_COLS + b_q] * B_QSCALE).astype(BF16)
    pb_ref[:, b_q:] = proj[:, A_COLS + b_q:A_COLS + B_COLS].astype(BF16)
    c = proj[:, A_COLS + B_COLS:]
    c_q = c[:, :C_Q_LORA]
    c_kv = c[:, C_Q_LORA:C_Q_LORA + C_KV_LORA]
    k_rot = c[:, C_Q_LORA + C_KV_LORA:C_Q_LORA + C_KV_LORA + LANES]
    k_rot_half = c[:, C_Q_LORA + C_KV_LORA + LANES:]
    cqn = (c_q * lax.rsqrt(jnp.mean(c_q * c_q, axis=-1, keepdims=True) + LATENT_EPS) * qg_ref[...]).astype(BF16)
    ckvn = (c_kv * lax.rsqrt(jnp.mean(c_kv * c_kv, axis=-1, keepdims=True) + LATENT_EPS) * kvg_ref[...]).astype(BF16)
    q2 = jnp.dot(cqn, wq_ref[...], preferred_element_type=F32)
    kv = jnp.dot(ckvn, wkv_ref[...], preferred_element_type=F32)
    cos = cos_ref[...]
    sin = sin_ref[...]
    k_rope = k_rot * cos + k_rot_half * sin
    qw = C_HEADS * LANES
    for h in range(C_HEADS):
        sl = slice(h * LANES, (h + 1) * LANES)
        sl2 = slice(qw + h * LANES, qw + (h + 1) * LANES)
        qc_ref[:, sl] = ((q2[:, sl] * cos + q2[:, sl2] * sin) * C_QSCALE).astype(BF16)
        kc_ref[:, sl] = (kv[:, sl] + k_rope).astype(BF16)
    vc_ref[...] = kv[:, qw:].astype(BF16)


def _projection(x2d, w1, qg, kvg, wq, wkv, cos_t, sin_t, seq):
    n_tok = x2d.shape[0]
    tm = PROJ_TM
    pos_blocks = seq // tm
    full = lambda shape: pl.BlockSpec(shape, lambda i: (0, 0))
    tok = lambda width: pl.BlockSpec((tm, width), lambda i: (i, 0))
    pos = pl.BlockSpec((tm, LANES), lambda i: (i % pos_blocks, 0))
    return pl.pallas_call(
        _proj_kernel,
        grid=(n_tok // tm,),
        in_specs=[tok(D_MODEL), full(w1.shape), full(qg.shape), full(kvg.shape), full(wq.shape),
                  full(wkv.shape), pos, pos],
        out_specs=[tok(A_COLS), tok(B_COLS), tok(C_HEADS * LANES), tok(C_HEADS * LANES),
                   tok(C_HEADS * HEAD_DIM)],
        out_shape=[jax.ShapeDtypeStruct((n_tok, A_COLS), F32),
                   jax.ShapeDtypeStruct((n_tok, B_COLS), BF16),
                   jax.ShapeDtypeStruct((n_tok, C_HEADS * LANES), BF16),
                   jax.ShapeDtypeStruct((n_tok, C_HEADS * LANES), BF16),
                   jax.ShapeDtypeStruct((n_tok, C_HEADS * HEAD_DIM), BF16)],
        compiler_params=pltpu.CompilerParams(dimension_semantics=("arbitrary",),
                                             vmem_limit_bytes=VMEM_LIMIT),
        name="projection",
    )(x2d, w1, qg, kvg, wq, wkv, cos_t, sin_t)


def _attn_a_kernel(q_ref, k_ref, v_ref, bias_ref, o_ref, d4_ref, d16_ref, r0_ref, r1c_ref, r1_ref,
                   r2c_ref, r2m_ref, r2_ref):
    seq = q_ref.shape[1]
    len4, len16 = seq // 4, seq // 16
    lane = lax.broadcasted_iota(jnp.int32, (A_BLK, LANES), 1)
    lo_half = lane < HEAD_DIM
    ones = jnp.ones((2 * A_BLK, LANES), BF16)

    for a, src in enumerate((q_ref, k_ref, v_ref)):
        for r4 in range(4):
            d4_ref[a, r4] = src[0, pl.ds(r4, len4, stride=4), :]
        for r4 in range(4):
            for hi in range(4):
                d16_ref[a, 4 * hi + r4] = d4_ref[a, r4, pl.ds(hi, len16, stride=4), :]

    def attend(qf, kf, vf, p, first):
        nk = kf.shape[0]
        kb = kf.astype(BF16)
        v_aug = jnp.concatenate([vf.astype(BF16), ones[:nk]], axis=1)
        out, lse = [], []
        for j in range(2):
            qj = jnp.where(lo_half if j == 0 else jnp.logical_not(lo_half), qf, 0.0).astype(BF16)
            bias = bias_ref[j, p]
            if first:
                bias = bias[:, A_BLK:]
            s = _nt_dot(qj, kb) + bias
            m = jnp.max(s, axis=-1, keepdims=True)
            pr = jnp.exp2(s - m).astype(BF16)
            r = jnp.dot(pr, v_aug, preferred_element_type=F32)
            den = r[:, LANES:]
            out.append(r[:, :LANES] / den)
            lse.append(m + jnp.log2(den))
        return jnp.where(lo_half, out[0], out[1]), jnp.where(lo_half, lse[0], lse[1])

    def key_rows(start, first):
        return pl.ds(start, A_BLK) if first else pl.ds(start - A_BLK, 2 * A_BLK)

    def pair0_block(start, first):
        kr = key_rows(start, first)
        out, lse = attend(q_ref[0, pl.ds(start, A_BLK), :], k_ref[0, kr, :], v_ref[0, kr, :], 0, first)
        r0_ref[0, pl.ds(start, A_BLK), :] = out
        r0_ref[1, pl.ds(start, A_BLK), :] = lse

    for n in range(seq // A_BLK):
        pair0_block(n * A_BLK, n == 0)

    def pair1_class(r4, c):
        for n in range(len4 // A_BLK):
            kr = key_rows(n * A_BLK, n == 0)
            out, lse = attend(d4_ref[0, r4, pl.ds(n * A_BLK, A_BLK), :], d4_ref[1, r4, kr, :],
                              d4_ref[2, r4, kr, :], 1, n == 0)
            r1c_ref[0, r4, pl.ds(n * A_BLK, A_BLK), :] = out
            r1c_ref[1, r4, pl.ds(n * A_BLK, A_BLK), :] = lse
        return c
    for r4 in range(4):
        pair1_class(r4, 0)

    def pair2_class(r, c):
        out, lse = attend(d16_ref[0, r], d16_ref[1, r], d16_ref[2, r], 2, True)
        r2c_ref[0, r] = out
        r2c_ref[1, r] = lse
        return c
    for r in range(16):
        pair2_class(r, 0)

    for c in range(2):
        for r4 in range(4):
            for hi in range(4):
                r2m_ref[c, r4, pl.ds(hi, len16, stride=4), :] = r2c_ref[c, 4 * hi + r4]
        for r4 in range(4):
            r1_ref[c, pl.ds(r4, len4, stride=4), :] = r1c_ref[c, r4]
            r2_ref[c, pl.ds(r4, len4, stride=4), :] = r2m_ref[c, r4]

    chunk = 256

    def combine(i, carry):
        rs = pl.ds(pl.multiple_of(i * chunk, chunk), chunk)
        l0, l1, l2 = r0_ref[1, rs, :], r1_ref[1, rs, :], r2_ref[1, rs, :]
        mx = jnp.maximum(jnp.maximum(l0, l1), l2)
        e0, e1, e2 = jnp.exp2(l0 - mx), jnp.exp2(l1 - mx), jnp.exp2(l2 - mx)
        num = e0 * r0_ref[0, rs, :] + e1 * r1_ref[0, rs, :] + e2 * r2_ref[0, rs, :]
        o_ref[0, rs, :] = (num / (e0 + e1 + e2)).astype(o_ref.dtype)
        return carry

    lax.fori_loop(0, seq // chunk, combine, 0)


def _attention_a(proj_a, bias_a):
    batch, seq, _ = proj_a.shape
    pairs = A_HEADS // 2
    n_pairs = len(A_PAIRS)
    col = lambda off: pl.BlockSpec((1, seq, LANES), lambda hp, b: (b, 0, off + hp))
    result = pltpu.VMEM((2, seq, LANES), F32)
    by4 = pltpu.VMEM((2, 4, seq // 4, LANES), F32)
    return pl.pallas_call(
        _attn_a_kernel,
        grid=(pairs, batch),
        in_specs=[col(0), col(pairs), col(2 * pairs),
                  pl.BlockSpec((2, n_pairs, A_BLK, 2 * A_BLK), lambda hp, b: (hp, 0, 0, 0))],
        out_specs=pl.BlockSpec((1, seq, LANES), lambda hp, b: (b, 0, hp)),
        out_shape=jax.ShapeDtypeStruct((batch, seq, A_HEADS * HEAD_DIM), BF16),
        scratch_shapes=[pltpu.VMEM((3, 4, seq // 4, LANES), F32), pltpu.VMEM((3, 16, seq // 16, LANES), F32),
                        result, by4, result, pltpu.VMEM((2, 16, seq // 16, LANES), F32), by4, result],
        compiler_params=pltpu.CompilerParams(dimension_semantics=("arbitrary", "arbitrary"),
                                             vmem_limit_bytes=VMEM_LIMIT),
        name="attention_a",
    )(proj_a, proj_a, proj_a, bias_a)


def _softmax_init(m_ref, acc_ref):
    m_ref[...] = jnp.full(m_ref.shape, MASK_VALUE, F32)
    acc_ref[...] = jnp.zeros(acc_ref.shape, F32)


def _softmax_step(s, v_aug, m_ref, acc_ref, c):
    rows = slice(BQ - s.shape[0], BQ)
    m_old = m_ref[c, rows]
    m_new = jnp.max(jnp.concatenate([s, m_old], axis=1), axis=-1, keepdims=True)
    alpha = jnp.exp2(m_old - m_new)
    pr = jnp.exp2(s - m_new).astype(BF16)
    acc_ref[c, rows] = acc_ref[c, rows] * alpha + jnp.dot(pr, v_aug, preferred_element_type=F32)
    m_ref[c, rows] = jnp.broadcast_to(m_new, m_old.shape)


def _attn_b_kernel(q_ref, k_ref, v_ref, bias_ref, lam_ref, g_ref, o_ref, m_ref, acc_ref, *, lam_init):
    qi = pl.program_id(2)
    q = q_ref[0]
    lane = lax.broadcasted_iota(jnp.int32, (BQ, LANES), 1)
    lf = lam_ref[...]
    dot_rows = lambda a, b: jnp.sum(lf[a:a + 1] * lf[b:b + 1], axis=-1, keepdims=True)
    lam = jnp.exp(dot_rows(0, 1)) - jnp.exp(dot_rows(2, 3)) + lam_init

    qs = jnp.concatenate(
        [jnp.where((lane >= c * B_QK_DIM) & (lane < (c + 1) * B_QK_DIM), q, jnp.zeros_like(q))
         for c in range(4)], axis=0)

    own = _own_head_lanes(BK)

    def step(ki, carry):
        off = pl.multiple_of(ki * BK, BK)
        k = k_ref[0, pl.ds(off, BK), :]
        v = v_ref[0, pl.ds(off, BK), :]
        v_aug = [_values_and_ones(v, own[j]) for j in range(2)]
        s_all = _nt_dot(qs, k)
        for c in range(4):
            bias = bias_ref[c // 2, qi - ki]
            _softmax_step(s_all[c * BQ:(c + 1) * BQ] + bias, v_aug[c // 2], m_ref, acc_ref, c)
        return carry

    _softmax_init(m_ref, acc_ref)
    lax.fori_loop(0, qi + 1, step, 0)

    outs = []
    for j in range(2):
        base = j * HEAD_DIM
        a1, a2 = acc_ref[2 * j], acc_ref[2 * j + 1]
        o = _divide_by_row_sum(a1) - lam * _divide_by_row_sum(a2)
        in_head = (lane >= base) & (lane < base + HEAD_DIM)
        ms = jnp.sum(jnp.where(in_head, o * o, 0.0), axis=-1, keepdims=True) * (1.0 / HEAD_DIM)
        outs.append(o * lax.rsqrt(ms + SUBLN_EPS) * g_ref[...] * (1.0 - lam_init))
    o_ref[0] = jnp.where(lane < HEAD_DIM, outs[0], outs[1]).astype(o_ref.dtype)


def _attention_b(proj_b, bias_b, diff_lambda, subln_g2, lam_init):
    batch, seq, _ = proj_b.shape
    pairs = B_HEADS // 2
    n_tiles = seq // BQ
    return pl.pallas_call(
        functools.partial(_attn_b_kernel, lam_init=lam_init),
        grid=(pairs, batch, n_tiles),
        in_specs=[pl.BlockSpec((1, BQ, LANES), lambda hp, b, i: (b, i, hp)),
                  pl.BlockSpec((1, seq, LANES), lambda hp, b, i: (b, 0, pairs + hp)),
                  pl.BlockSpec((1, seq, LANES), lambda hp, b, i: (b, 0, 2 * pairs + hp)),
                  pl.BlockSpec((2, seq // BK, BQ, BK), lambda hp, b, i: (hp, 0, 0, 0),
                               pipeline_mode=pl.Buffered(1)),
                  pl.BlockSpec(diff_lambda.shape, lambda hp, b, i: (0, 0)),
                  pl.BlockSpec(subln_g2.shape, lambda hp, b, i: (0, 0))],
        out_specs=pl.BlockSpec((1, BQ, LANES), lambda hp, b, i: (b, i, hp)),
        out_shape=jax.ShapeDtypeStruct((batch, seq, B_HEADS * HEAD_DIM), BF16),
        scratch_shapes=[pltpu.VMEM((4, BQ, LANES), F32)] * 2,
        compiler_params=pltpu.CompilerParams(dimension_semantics=("arbitrary",) * 3,
                                             vmem_limit_bytes=VMEM_LIMIT),
        name="attention_b",
    )(proj_b, proj_b, proj_b, bias_b, diff_lambda, subln_g2)


def _attn_c_kernel(q_ref, k_ref, v_ref, o_ref, m_ref, acc_ref):
    qi = pl.program_id(2)
    lane = lax.broadcasted_iota(jnp.int32, (BQ, LANES), 1)
    def key_tiles(ki, n):
        off = pl.multiple_of(ki * BK, BK)
        v = v_ref[0, pl.ds(off, n * BK), :]
        own = _own_head_lanes(n * BK)
        for j in range(2):
            hs = slice(j * LANES, (j + 1) * LANES)
            s = _nt_dot(q_ref[0, :, hs], k_ref[0, pl.ds(off, n * BK), hs])
            _softmax_step(s, _values_and_ones(v, own[j]), m_ref, acc_ref, j)

    _softmax_init(m_ref, acc_ref)

    def pair(t, carry):
        key_tiles(2 * t, 2)
        return carry
    lax.fori_loop(0, qi >> 1, pair, 0)

    @pl.when((qi & 1) == 1)
    def _():
        key_tiles(qi - 1, 1)

    diag = pl.multiple_of(qi * BK, BK)
    own_half = _own_head_lanes(HALF)
    for q_rows, k_lo in ((BQ, 0), (HALF, HALF)):
        row = lax.broadcasted_iota(jnp.int32, (q_rows, HALF), 0)
        colm = lax.broadcasted_iota(jnp.int32, (q_rows, HALF), 1)
        causal = jnp.where(row >= colm, 0.0, MASK_VALUE)
        v = v_ref[0, pl.ds(diag + k_lo, HALF), :]
        for j in range(2):
            hs = slice(j * LANES, (j + 1) * LANES)
            s = _nt_dot(q_ref[0, BQ - q_rows:, hs], k_ref[0, pl.ds(diag + k_lo, HALF), hs]) + causal
            _softmax_step(s, _values_and_ones(v, own_half[j]), m_ref, acc_ref, j)
    o_ref[0] = jnp.where(lane < HEAD_DIM, _divide_by_row_sum(acc_ref[0]),
                         _divide_by_row_sum(acc_ref[1])).astype(o_ref.dtype)


def _attention_c(qc, kc, vc):
    batch, seq, _ = qc.shape
    pairs = C_HEADS // 2
    n_tiles = seq // BQ
    return pl.pallas_call(
        _attn_c_kernel,
        grid=(pairs, batch, n_tiles),
        in_specs=[pl.BlockSpec((1, BQ, 2 * LANES), lambda hp, b, i: (b, i, hp)),
                  pl.BlockSpec((1, seq, 2 * LANES), lambda hp, b, i: (b, 0, hp)),
                  pl.BlockSpec((1, seq, LANES), lambda hp, b, i: (b, 0, hp))],
        out_specs=pl.BlockSpec((1, BQ, LANES), lambda hp, b, i: (b, i, hp)),
        out_shape=jax.ShapeDtypeStruct((batch, seq, C_HEADS * HEAD_DIM), BF16),
        scratch_shapes=[pltpu.VMEM((2, BQ, LANES), F32)] * 2,
        compiler_params=pltpu.CompilerParams(dimension_semantics=("arbitrary",) * 3,
                                             vmem_limit_bytes=VMEM_LIMIT),
        name="attention_c",
    )(qc, kc, vc)


def _layer_norm(y, g, b):
    mu = jnp.mean(y, axis=-1, keepdims=True)
    yc = y - mu
    var = jnp.mean(yc * yc, axis=-1, keepdims=True)
    return yc * lax.rsqrt(var + LN_EPS) * g + b


def _out_ffn_kernel(x_ref, oa_ref, ob_ref, oc_ref, wo_ref, g1_ref, b1_ref, g2_ref, b2_ref,
                    wg_ref, wu_ref, wd_ref, out_ref, *, alpha):
    heads = jnp.concatenate([oa_ref[...], ob_ref[...], oc_ref[...]], axis=1)
    mix = jnp.dot(heads, wo_ref[...], preferred_element_type=F32)
    h = _layer_norm(alpha * x_ref[...] + mix, g1_ref[...], b1_ref[...])
    hb = h.astype(BF16)
    ffn = jnp.zeros(h.shape, F32)
    for lo, hi in zip(FFN_CHUNK_EDGES[:-1], FFN_CHUNK_EDGES[1:]):
        cs = slice(lo, hi)
        gate = jnp.dot(hb, wg_ref[:, cs], preferred_element_type=F32)
        up = jnp.dot(hb, wu_ref[:, cs], preferred_element_type=F32)
        act = (jax.nn.silu(gate) * up).astype(BF16)
        ffn = ffn + jnp.dot(act, wd_ref[cs, :], preferred_element_type=F32)
    out_ref[...] = _layer_norm(alpha * h + ffn, g2_ref[...], b2_ref[...])


def _out_ffn(x2d, oa, ob, oc, wo, g1, b1, g2, b2, wg, wu, wd, alpha):
    n_tok = x2d.shape[0]
    tm = FFN_TM
    tok = lambda width: pl.BlockSpec((tm, width), lambda i: (i, 0))
    once = lambda a: pl.BlockSpec(a.shape, lambda i: (0, 0), pipeline_mode=pl.Buffered(1))
    return pl.pallas_call(
        functools.partial(_out_ffn_kernel, alpha=alpha),
        grid=(n_tok // tm,),
        in_specs=[tok(D_MODEL), tok(oa.shape[1]), tok(ob.shape[1]), tok(oc.shape[1]),
                  once(wo), once(g1), once(b1), once(g2), once(b2), once(wg), once(wu), once(wd)],
        out_specs=tok(D_MODEL),
        out_shape=jax.ShapeDtypeStruct((n_tok, D_MODEL), F32),
        compiler_params=pltpu.CompilerParams(dimension_semantics=("arbitrary",),
                                             vmem_limit_bytes=VMEM_LIMIT),
        name="out_ffn",
    )(x2d, oa, ob, oc, wo, g1, b1, g2, b2, wg, wu, wd)


def _rotate_half_cols(w):
    half = w.shape[-1] // 2
    return jnp.concatenate([-w[..., half:], w[..., :half]], axis=-1)


def _place_rope(w):
    return jnp.pad(w, ((0, 0), (C_NOPE, LANES - C_NOPE - C_ROPE)))


def _layer_weights(w_in, w_uq, w_ukv):
    main = A_COLS + B_COLS + C_Q_LORA + C_KV_LORA
    k_r = w_in[:, main:]
    w1 = jnp.concatenate([w_in[:, :main], _place_rope(k_r), _place_rope(_rotate_half_cols(k_r))],
                         axis=1).astype(BF16)
    rows = w_uq.shape[0]
    uq = w_uq.reshape(rows, C_HEADS, C_NOPE + C_ROPE)
    nope, rope = uq[..., :C_NOPE], uq[..., C_NOPE:]
    tail = jnp.zeros((rows, C_HEADS, LANES - C_NOPE - C_ROPE), w_uq.dtype)
    q_main = jnp.concatenate([nope, rope, tail], axis=-1).reshape(rows, C_HEADS * LANES)
    q_half = jnp.concatenate([jnp.zeros_like(nope), _rotate_half_cols(rope), tail],
                             axis=-1).reshape(rows, C_HEADS * LANES)
    wq = jnp.concatenate([q_main, q_half], axis=1).astype(BF16)
    rows = w_ukv.shape[0]
    ukv = w_ukv.reshape(rows, C_HEADS, C_NOPE + HEAD_DIM)
    k_nope, v = ukv[..., :C_NOPE], ukv[..., C_NOPE:]
    k_main = jnp.concatenate([k_nope, jnp.zeros((rows, C_HEADS, LANES - C_NOPE), w_ukv.dtype)],
                             axis=-1).reshape(rows, C_HEADS * LANES)
    wkv = jnp.concatenate([k_main, v.reshape(rows, C_HEADS * HEAD_DIM)], axis=1).astype(BF16)
    return w1, wq, wkv


def _rope_tables(seq):
    half = C_ROPE // 2
    inv = ROPE_THETA ** (-np.arange(half, dtype=np.float64) / half)
    ang = np.arange(seq, dtype=np.float64)[:, None] * inv[None, :]
    cos = np.zeros((seq, LANES), np.float64)
    sin = np.zeros((seq, LANES), np.float64)
    cos[:, :C_NOPE] = 1.0
    cos[:, C_NOPE:C_NOPE + C_ROPE] = np.concatenate([np.cos(ang), np.cos(ang)], axis=1)
    sin[:, C_NOPE:C_NOPE + C_ROPE] = np.concatenate([np.sin(ang), np.sin(ang)], axis=1)
    return jnp.asarray(cos, F32), jnp.asarray(sin, F32)


def kernel(x, rel_bias, w_in, q_norm_g, kv_norm_g, w_uq, w_ukv, diff_lambda, subln_g, w_o, ln1_g, ln1_b, ln2_g, ln2_b, w_gate, w_up, w_down):
    batch, seq, d_model = x.shape
    depth = w_in.shape[0]
    alpha = (2 * depth) ** 0.25
    bias_a = _bias_tiles(rel_bias, A_HEADS, 0, len(A_PAIRS), A_BLK, 2 * A_BLK, True)
    bias_b = _bias_tiles(rel_bias, B_HEADS, A_HEADS, seq // BK, BQ, BK, False)
    cos_t, sin_t = _rope_tables(seq)
    row = lambda a: a.reshape(1, -1)
    x2d = x.reshape(batch * seq, d_model)
    for l in range(depth):
        w1, wq, wkv = _layer_weights(w_in[l], w_uq[l], w_ukv[l])
        pa, pb, qc, kc, vc = _projection(x2d, w1, row(q_norm_g[l]), row(kv_norm_g[l]), wq, wkv,
                                         cos_t, sin_t, seq)
        lam_init = 0.8 - 0.6 * math.exp(-0.3 * l)
        oa = _attention_a(pa.reshape(batch, seq, -1), bias_a)
        ob = _attention_b(pb.reshape(batch, seq, -1), bias_b, diff_lambda[l],
                          row(jnp.concatenate([subln_g[l], subln_g[l]])), lam_init)
        oc = _attention_c(qc.reshape(batch, seq, -1), kc.reshape(batch, seq, -1),
                          vc.reshape(batch, seq, -1))
        flat = lambda a: a.reshape(batch * seq, -1)
        x2d = _out_ffn(x2d, flat(oa), flat(ob), flat(oc), w_o[l].astype(BF16),
                       row(ln1_g[l]), row(ln1_b[l]), row(ln2_g[l]), row(ln2_b[l]),
                       w_gate[l].astype(BF16), w_up[l].astype(BF16), w_down[l].astype(BF16), alpha)
    return x2d.reshape(batch, seq, d_model)
```

```python
import functools
import math

import numpy as np
import jax
import jax.numpy as jnp
from jax import lax
from jax.experimental import pallas as pl
from jax.experimental.pallas import tpu as pltpu

F32 = jnp.float32
BF16 = jnp.bfloat16

D_MODEL = 1024
HEAD_DIM = 64
A_HEADS = 6
A_PAIRS = ((128, 1), (512, 4), (2048, 16))
A_BLK = 128
B_HEADS = 4
B_QK_DIM = 32
C_HEADS = 6
C_Q_LORA = 256
C_KV_LORA = 128
C_NOPE = 64
C_ROPE = 32
ROPE_THETA = 10000.0
A_COLS = 3 * A_HEADS * HEAD_DIM
B_COLS = B_HEADS * 3 * HEAD_DIM
NUM_BUCKETS = 32
MAX_DISTANCE = 2048
FF_DIM = 2816
LN_EPS = 1e-5
LATENT_EPS = 1e-6
SUBLN_EPS = 1e-5

LANES = 128
MASK_VALUE = -1e30
LOG2E = math.log2(math.e)
A_QSCALE = HEAD_DIM ** -0.5 * LOG2E
B_QSCALE = B_QK_DIM ** -0.5 * LOG2E
C_QSCALE = (C_NOPE + C_ROPE) ** -0.5 * LOG2E
VMEM_LIMIT = 56 * 1024 * 1024

PROJ_TM = 512
FFN_TM = 512
CAST_ROWS = 256
MXU_WIDTH = 256
FFN_CHUNK_EDGES = (0, 5 * MXU_WIDTH, FF_DIM)
BQ = 512
BK = BQ
HALF = BK // 2


def _bucket_lower_bounds():
    max_exact = NUM_BUCKETS // 2
    d = np.arange(0, MAX_DISTANCE + 1)
    val = np.log(np.maximum(d, 1) / max_exact) / math.log(MAX_DISTANCE / max_exact) * (NUM_BUCKETS - max_exact)
    large = np.minimum(max_exact + np.floor(np.maximum(val, 0.0)).astype(np.int64), NUM_BUCKETS - 1)
    bucket = np.where(d < max_exact, d, large)
    return [int(np.argmax(bucket >= b)) for b in range(NUM_BUCKETS)]


_BUCKET_LO = _bucket_lower_bounds()


def _nt_dot(a, b):
    return lax.dot_general(a, b, (((1,), (1,)), ((), ())), preferred_element_type=F32)


def _own_head_lanes(rows):
    lane = lax.broadcasted_iota(jnp.int32, (rows, LANES), 1)
    return lane < HEAD_DIM, lane >= HEAD_DIM


def _values_and_ones(v, own):
    return jnp.where(own, v, jnp.ones_like(v))


def _divide_by_row_sum(acc):
    return acc / pltpu.roll(acc, HEAD_DIM, 1)


def _bias_tile_kernel(rb_ref, out_ref, *, head_offset, dilated):
    h = pl.program_id(0) + head_offset
    g = pl.program_id(1)
    rows, cols = out_ref.shape[2], out_ref.shape[3]
    i = lax.broadcasted_iota(jnp.int32, (rows, cols), 0)
    j = lax.broadcasted_iota(jnp.int32, (rows, cols), 1)
    if dilated:
        rel = i + A_BLK - j
        valid = (rel >= 0) & (rel <= A_BLK)
        dil = jnp.where(g == 0, A_PAIRS[0][1], jnp.where(g == 1, A_PAIRS[1][1], A_PAIRS[2][1]))
        dist = rel * dil
    else:
        dist = g * cols + i - j
        valid = dist >= 0
    val = jnp.full((rows, cols), rb_ref[0, h], F32)
    for b in range(1, NUM_BUCKETS):
        val = jnp.where(dist >= _BUCKET_LO[b], rb_ref[b, h], val)
    out_ref[0, 0] = jnp.where(valid, val * LOG2E, MASK_VALUE)


def _bias_tiles(rel_bias, n_heads, head_offset, n_groups, rows, cols, dilated):
    return pl.pallas_call(
        functools.partial(_bias_tile_kernel, head_offset=head_offset, dilated=dilated),
        grid=(n_heads, n_groups),
        in_specs=[pl.BlockSpec(memory_space=pltpu.SMEM)],
        out_specs=pl.BlockSpec((1, 1, rows, cols), lambda h, g: (h, g, 0, 0)),
        out_shape=jax.ShapeDtypeStruct((n_heads, n_groups, rows, cols), F32),
        name="bias_tiles",
    )(rel_bias)


def _proj_kernel(x_ref, w1_ref, qg_ref, kvg_ref, wq_ref, wkv_ref, cos_ref, sin_ref,
                 pa_ref, pb_ref, qc_ref, kc_ref, vc_ref):
    xb = x_ref[...].astype(BF16)
    proj = jnp.dot(xb, w1_ref[...], preferred_element_type=F32)
    a_q = A_HEADS * HEAD_DIM
    pa_ref[:, :a_q] = proj[:, :a_q] * A_QSCALE
    pa_ref[:, a_q:] = proj[:, a_q:A_COLS]
    b_q = B_HEADS * HEAD_DIM
    pb_ref[:, :b_q] = (proj[:, A_COLS:A_COLS + b_q] * B_QSCALE).astype(BF16)
    pb_ref[:, b_q:] = proj[:, A_COLS + b_q:A_COLS + B_COLS].astype(BF16)
    c = proj[:, A_COLS + B_COLS:]
    c_q = c[:, :C_Q_LORA]
    c_kv = c[:, C_Q_LORA:C_Q_LORA + C_KV_LORA]
    k_rot = c[:, C_Q_LORA + C_KV_LORA:C_Q_LORA + C_KV_LORA + LANES]
    k_rot_half = c[:, C_Q_LORA + C_KV_LORA + LANES:]
    cqn = (c_q * lax.rsqrt(jnp.mean(c_q * c_q, axis=-1, keepdims=True) + LATENT_EPS) * qg_ref[...]).astype(BF16)
    ckvn = (c_kv * lax.rsqrt(jnp.mean(c_kv * c_kv, axis=-1, keepdims=True) + LATENT_EPS) * kvg_ref[...]).astype(BF16)
    q2 = jnp.dot(cqn, wq_ref[...], preferred_element_type=F32)
    kv = jnp.dot(ckvn, wkv_ref[...], preferred_element_type=F32)
    cos = cos_ref[...]
    sin = sin_ref[...]
    k_rope = k_rot * cos + k_rot_half * sin
    qw = C_HEADS * LANES
    for h in range(C_HEADS):
        sl = slice(h * LANES, (h + 1) * LANES)
        sl2 = slice(qw + h * LANES, qw + (h + 1) * LANES)
        qc_ref[:, sl] = ((q2[:, sl] * cos + q2[:, sl2] * sin) * C_QSCALE).astype(BF16)
        kc_ref[:, sl] = (kv[:, sl] + k_rope).astype(BF16)
    vc_ref[...] = kv[:, qw:].astype(BF16)


def _projection(x2d, w1, qg, kvg, wq, wkv, cos_t, sin_t, seq):
    n_tok = x2d.shape[0]
    tm = PROJ_TM
    pos_blocks = seq // tm
    full = lambda shape: pl.BlockSpec(shape, lambda i: (0, 0))
    tok = lambda width: pl.BlockSpec((tm, width), lambda i: (i, 0))
    pos = pl.BlockSpec((tm, LANES), lambda i: (i % pos_blocks, 0))
    return pl.pallas_call(
        _proj_kernel,
        grid=(n_tok // tm,),
        in_specs=[tok(D_MODEL), full(w1.shape), full(qg.shape), full(kvg.shape), full(wq.shape),
                  full(wkv.shape), pos, pos],
        out_specs=[tok(A_COLS), tok(B_COLS), tok(C_HEADS * LANES), tok(C_HEADS * LANES),
                   tok(C_HEADS * HEAD_DIM)],
        out_shape=[jax.ShapeDtypeStruct((n_tok, A_COLS), F32),
                   jax.ShapeDtypeStruct((n_tok, B_COLS), BF16),
                   jax.ShapeDtypeStruct((n_tok, C_HEADS * LANES), BF16),
                   jax.ShapeDtypeStruct((n_tok, C_HEADS * LANES), BF16),
                   jax.ShapeDtypeStruct((n_tok, C_HEADS * HEAD_DIM), BF16)],
        compiler_params=pltpu.CompilerParams(dimension_semantics=("arbitrary",),
                                             vmem_limit_bytes=VMEM_LIMIT),
        name="projection",
    )(x2d, w1, qg, kvg, wq, wkv, cos_t, sin_t)


def _attn_a_kernel(q_ref, k_ref, v_ref, bias_ref, o_ref, d4_ref, d16_ref, r0_ref, r1c_ref, r1_ref,
                   r2c_ref, r2m_ref, r2_ref):
    seq = q_ref.shape[1]
    len4, len16 = seq // 4, seq // 16
    lane = lax.broadcasted_iota(jnp.int32, (A_BLK, LANES), 1)
    lo_half = lane < HEAD_DIM
    ones = jnp.ones((2 * A_BLK, LANES), BF16)

    for a, src in enumerate((q_ref, k_ref, v_ref)):
        for r4 in range(4):
            d4_ref[a, r4] = src[0, pl.ds(r4, len4, stride=4), :]
        for r4 in range(4):
            for hi in range(4):
                d16_ref[a, 4 * hi + r4] = d4_ref[a, r4, pl.ds(hi, len16, stride=4), :]

    def attend(qf, kf, vf, p, first):
        nk = kf.shape[0]
        kb = kf.astype(BF16)
        v_aug = jnp.concatenate([vf.astype(BF16), ones[:nk]], axis=1)
        out, lse = [], []
        for j in range(2):
            qj = jnp.where(lo_half if j == 0 else jnp.logical_not(lo_half), qf, 0.0).astype(BF16)
            bias = bias_ref[j, p]
            if first:
                bias = bias[:, A_BLK:]
            s = _nt_dot(qj, kb) + bias
            m = jnp.max(s, axis=-1, keepdims=True)
            pr = jnp.exp2(s - m).astype(BF16)
            r = jnp.dot(pr, v_aug, preferred_element_type=F32)
            den = r[:, LANES:]
            out.append(r[:, :LANES] / den)
            lse.append(m + jnp.log2(den))
        return jnp.where(lo_half, out[0], out[1]), jnp.where(lo_half, lse[0], lse[1])

    def key_rows(start, first):
        return pl.ds(start, A_BLK) if first else pl.ds(start - A_BLK, 2 * A_BLK)

    def pair0_block(start, first):
        kr = key_rows(start, first)
        out, lse = attend(q_ref[0, pl.ds(start, A_BLK), :], k_ref[0, kr, :], v_ref[0, kr, :], 0, first)
        r0_ref[0, pl.ds(start, A_BLK), :] = out
        r0_ref[1, pl.ds(start, A_BLK), :] = lse

    for n in range(seq // A_BLK):
        pair0_block(n * A_BLK, n == 0)

    def pair1_class(r4, c):
        for n in range(len4 // A_BLK):
            kr = key_rows(n * A_BLK, n == 0)
            out, lse = attend(d4_ref[0, r4, pl.ds(n * A_BLK, A_BLK), :], d4_ref[1, r4, kr, :],
                              d4_ref[2, r4, kr, :], 1, n == 0)
            r1c_ref[0, r4, pl.ds(n * A_BLK, A_BLK), :] = out
            r1c_ref[1, r4, pl.ds(n * A_BLK, A_BLK), :] = lse
        return c
    for r4 in range(4):
        pair1_class(r4, 0)

    def pair2_class(r, c):
        out, lse = attend(d16_ref[0, r], d16_ref[1, r], d16_ref[2, r], 2, True)
        r2c_ref[0, r] = out
        r2c_ref[1, r] = lse
        return c
    for r in range(16):
        pair2_class(r, 0)

    for c in range(2):
        for r4 in range(4):
            for hi in range(4):
                r2m_ref[c, r4, pl.ds(hi, len16, stride=4), :] = r2c_ref[c, 4 * hi + r4]
        for r4 in range(4):
            r1_ref[c, pl.ds(r4, len4, stride=4), :] = r1c_ref[c, r4]
            r2_ref[c, pl.ds(r4, len4, stride=4), :] = r2m_ref[c, r4]

    chunk = 256

    def combine(i, carry):
        rs = pl.ds(pl.multiple_of(i * chunk, chunk), chunk)
        l0, l1, l2 = r0_ref[1, rs, :], r1_ref[1, rs, :], r2_ref[1, rs, :]
        mx = jnp.maximum(jnp.maximum(l0, l1), l2)
        e0, e1, e2 = jnp.exp2(l0 - mx), jnp.exp2(l1 - mx), jnp.exp2(l2 - mx)
        num = e0 * r0_ref[0, rs, :] + e1 * r1_ref[0, rs, :] + e2 * r2_ref[0, rs, :]
        o_ref[0, rs, :] = (num / (e0 + e1 + e2)).astype(o_ref.dtype)
        return carry

    lax.fori_loop(0, seq // chunk, combine, 0)


def _attention_a(proj_a, bias_a):
    batch, seq, _ = proj_a.shape
    pairs = A_HEADS // 2
    n_pairs = len(A_PAIRS)
    col = lambda off: pl.BlockSpec((1, seq, LANES), lambda hp, b: (b, 0, off + hp))
    result = pltpu.VMEM((2, seq, LANES), F32)
    by4 = pltpu.VMEM((2, 4, seq // 4, LANES), F32)
    return pl.pallas_call(
        _attn_a_kernel,
        grid=(pairs, batch),
        in_specs=[col(0), col(pairs), col(2 * pairs),
                  pl.BlockSpec((2, n_pairs, A_BLK, 2 * A_BLK), lambda hp, b: (hp, 0, 0, 0))],
        out_specs=pl.BlockSpec((1, seq, LANES), lambda hp, b: (b, 0, hp)),
        out_shape=jax.ShapeDtypeStruct((batch, seq, A_HEADS * HEAD_DIM), BF16),
        scratch_shapes=[pltpu.VMEM((3, 4, seq // 4, LANES), F32), pltpu.VMEM((3, 16, seq // 16, LANES), F32),
                        result, by4, result, pltpu.VMEM((2, 16, seq // 16, LANES), F32), by4, result],
        compiler_params=pltpu.CompilerParams(dimension_semantics=("arbitrary", "arbitrary"),
                                             vmem_limit_bytes=VMEM_LIMIT),
        name="attention_a",
    )(proj_a, proj_a, proj_a, bias_a)


def _softmax_init(m_ref, acc_ref):
    m_ref[...] = jnp.full(m_ref.shape, MASK_VALUE, F32)
    acc_ref[...] = jnp.zeros(acc_ref.shape, F32)


def _softmax_step(s, v_aug, m_ref, acc_ref, c):
    rows = slice(BQ - s.shape[0], BQ)
    m_old = m_ref[c, rows]
    m_new = jnp.max(jnp.concatenate([s, m_old], axis=1), axis=-1, keepdims=True)
    alpha = jnp.exp2(m_old - m_new)
    pr = jnp.exp2(s - m_new).astype(BF16)
    acc_ref[c, rows] = acc_ref[c, rows] * alpha + jnp.dot(pr, v_aug, preferred_element_type=F32)
    m_ref[c, rows] = jnp.broadcast_to(m_new, m_old.shape)


def _attn_b_kernel(q_ref, k_ref, v_ref, bias_ref, lam_ref, g_ref, o_ref, m_ref, acc_ref, *, lam_init):
    qi = pl.program_id(2)
    q = q_ref[0]
    lane = lax.broadcasted_iota(jnp.int32, (BQ, LANES), 1)
    lf = lam_ref[...]
    dot_rows = lambda a, b: jnp.sum(lf[a:a + 1] * lf[b:b + 1], axis=-1, keepdims=True)
    lam = jnp.exp(dot_rows(0, 1)) - jnp.exp(dot_rows(2, 3)) + lam_init

    qs = jnp.concatenate(
        [jnp.where((lane >= c * B_QK_DIM) & (lane < (c + 1) * B_QK_DIM), q, jnp.zeros_like(q))
         for c in range(4)], axis=0)

    own = _own_head_lanes(BK)

    def step(ki, carry):
        off = pl.multiple_of(ki * BK, BK)
        k = k_ref[0, pl.ds(off, BK), :]
        v = v_ref[0, pl.ds(off, BK), :]
        v_aug = [_values_and_ones(v, own[j]) for j in range(2)]
        s_all = _nt_dot(qs, k)
        for c in range(4):
            bias = bias_ref[c // 2, qi - ki]
            _softmax_step(s_all[c * BQ:(c + 1) * BQ] + bias, v_aug[c // 2], m_ref, acc_ref, c)
        return carry

    _softmax_init(m_ref, acc_ref)
    lax.fori_loop(0, qi + 1, step, 0)

    outs = []
    for j in range(2):
        base = j * HEAD_DIM
        a1, a2 = acc_ref[2 * j], acc_ref[2 * j + 1]
        o = _divide_by_row_sum(a1) - lam * _divide_by_row_sum(a2)
        in_head = (lane >= base) & (lane < base + HEAD_DIM)
        ms = jnp.sum(jnp.where(in_head, o * o, 0.0), axis=-1, keepdims=True) * (1.0 / HEAD_DIM)
        outs.append(o * lax.rsqrt(ms + SUBLN_EPS) * g_ref[...] * (1.0 - lam_init))
    o_ref[0] = jnp.where(lane < HEAD_DIM, outs[0], outs[1]).astype(o_ref.dtype)


def _attention_b(proj_b, bias_b, diff_lambda, subln_g2, lam_init):
    batch, seq, _ = proj_b.shape
    pairs = B_HEADS // 2
    n_tiles = seq // BQ
    return pl.pallas_call(
        functools.partial(_attn_b_kernel, lam_init=lam_init),
        grid=(pairs, batch, n_tiles),
        in_specs=[pl.BlockSpec((1, BQ, LANES), lambda hp, b, i: (b, i, hp)),
                  pl.BlockSpec((1, seq, LANES), lambda hp, b, i: (b, 0, pairs + hp)),
                  pl.BlockSpec((1, seq, LANES), lambda hp, b, i: (b, 0, 2 * pairs + hp)),
                  pl.BlockSpec((2, seq // BK, BQ, BK), lambda hp, b, i: (hp, 0, 0, 0),
                               pipeline_mode=pl.Buffered(1)),
                  pl.BlockSpec(diff_lambda.shape, lambda hp, b, i: (0, 0)),
                  pl.BlockSpec(subln_g2.shape, lambda hp, b, i: (0, 0))],
        out_specs=pl.BlockSpec((1, BQ, LANES), lambda hp, b, i: (b, i, hp)),
        out_shape=jax.ShapeDtypeStruct((batch, seq, B_HEADS * HEAD_DIM), BF16),
        scratch_shapes=[pltpu.VMEM((4, BQ, LANES), F32)] * 2,
        compiler_params=pltpu.CompilerParams(dimension_semantics=("arbitrary",) * 3,
                                             vmem_limit_bytes=VMEM_LIMIT),
        name="attention_b",
    )(proj_b, proj_b, proj_b, bias_b, diff_lambda, subln_g2)


def _attn_c_kernel(q_ref, k_ref, v_ref, o_ref, m_ref, acc_ref):
    qi = pl.program_id(2)
    lane = lax.broadcasted_iota(jnp.int32, (BQ, LANES), 1)
    def key_tiles(ki, n):
        off = pl.multiple_of(ki * BK, BK)
        v = v_ref[0, pl.ds(off, n * BK), :]
        own = _own_head_lanes(n * BK)
        for j in range(2):
            hs = slice(j * LANES, (j + 1) * LANES)
            s = _nt_dot(q_ref[0, :, hs], k_ref[0, pl.ds(off, n * BK), hs])
            _softmax_step(s, _values_and_ones(v, own[j]), m_ref, acc_ref, j)

    _softmax_init(m_ref, acc_ref)

    def pair(t, carry):
        key_tiles(2 * t, 2)
        return carry
    lax.fori_loop(0, qi >> 1, pair, 0)

    @pl.when((qi & 1) == 1)
    def _():
        key_tiles(qi - 1, 1)

    diag = pl.multiple_of(qi * BK, BK)
    own_half = _own_head_lanes(HALF)
    for q_rows, k_lo in ((BQ, 0), (HALF, HALF)):
        row = lax.broadcasted_iota(jnp.int32, (q_rows, HALF), 0)
        colm = lax.broadcasted_iota(jnp.int32, (q_rows, HALF), 1)
        causal = jnp.where(row >= colm, 0.0, MASK_VALUE)
        v = v_ref[0, pl.ds(diag + k_lo, HALF), :]
        for j in range(2):
            hs = slice(j * LANES, (j + 1) * LANES)
            s = _nt_dot(q_ref[0, BQ - q_rows:, hs], k_ref[0, pl.ds(diag + k_lo, HALF), hs]) + causal
            _softmax_step(s, _values_and_ones(v, own_half[j]), m_ref, acc_ref, j)
    o_ref[0] = jnp.where(lane < HEAD_DIM, _divide_by_row_sum(acc_ref[0]),
                         _divide_by_row_sum(acc_ref[1])).astype(o_ref.dtype)


def _attention_c(qc, kc, vc):
    batch, seq, _ = qc.shape
    pairs = C_HEADS // 2
    n_tiles = seq // BQ
    return pl.pallas_call(
        _attn_c_kernel,
        grid=(pairs, batch, n_tiles),
        in_specs=[pl.BlockSpec((1, BQ, 2 * LANES), lambda hp, b, i: (b, i, hp)),
                  pl.BlockSpec((1, seq, 2 * LANES), lambda hp, b, i: (b, 0, hp)),
                  pl.BlockSpec((1, seq, LANES), lambda hp, b, i: (b, 0, hp))],
        out_specs=pl.BlockSpec((1, BQ, LANES), lambda hp, b, i: (b, i, hp)),
        out_shape=jax.ShapeDtypeStruct((batch, seq, C_HEADS * HEAD_DIM), BF16),
        scratch_shapes=[pltpu.VMEM((2, BQ, LANES), F32)] * 2,
        compiler_params=pltpu.CompilerParams(dimension_semantics=("arbitrary",) * 3,
                                             vmem_limit_bytes=VMEM_LIMIT),
        name="attention_c",
    )(qc, kc, vc)


def _layer_norm(y, g, b):
    mu = jnp.mean(y, axis=-1, keepdims=True)
    yc = y - mu
    var = jnp.mean(yc * yc, axis=-1, keepdims=True)
    return yc * lax.rsqrt(var + LN_EPS) * g + b


def _out_ffn_kernel(x_ref, oa_ref, ob_ref, oc_ref, wo_ref, g1_ref, b1_ref, g2_ref, b2_ref,
                    wg_ref, wu_ref, wd_ref, out_ref, *, alpha):
    heads = jnp.concatenate([oa_ref[...], ob_ref[...], oc_ref[...]], axis=1)
    mix = jnp.dot(heads, wo_ref[...], preferred_element_type=F32)
    h = _layer_norm(alpha * x_ref[...] + mix, g1_ref[...], b1_ref[...])
    hb = h.astype(BF16)
    ffn = jnp.zeros(h.shape, F32)
    for lo, hi in zip(FFN_CHUNK_EDGES[:-1], FFN_CHUNK_EDGES[1:]):
        cs = slice(lo, hi)
        gate = jnp.dot(hb, wg_ref[:, cs], preferred_element_type=F32)
        up = jnp.dot(hb, wu_ref[:, cs], preferred_element_type=F32)
        act = (jax.nn.silu(gate) * up).astype(BF16)
        ffn = ffn + jnp.dot(act, wd_ref[cs, :], preferred_element_type=F32)
    out_ref[...] = _layer_norm(alpha * h + ffn, g2_ref[...], b2_ref[...])


def _cast_kernel(x_ref, o_ref):
    o_ref[...] = x_ref[...].astype(o_ref.dtype)


def _to_bf16(w):
    depth, rows, cols = w.shape
    spec = pl.BlockSpec((1, CAST_ROWS, cols), lambda d, i: (d, i, 0))
    return pl.pallas_call(
        _cast_kernel,
        grid=(depth, rows // CAST_ROWS),
        in_specs=[spec],
        out_specs=spec,
        out_shape=jax.ShapeDtypeStruct(w.shape, BF16),
        name="cast_bf16",
    )(w)


def _out_ffn(x2d, oa, ob, oc, layer, wo, g1, b1, g2, b2, wg, wu, wd, alpha):
    n_tok = x2d.shape[0]
    tm = FFN_TM
    tok = lambda width: pl.BlockSpec((tm, width), lambda i: (i, 0))
    once = lambda a: pl.BlockSpec(a.shape, lambda i: (0, 0), pipeline_mode=pl.Buffered(1))
    of_layer = lambda a: pl.BlockSpec((None,) + a.shape[1:], lambda i: (layer, 0, 0),
                                      pipeline_mode=pl.Buffered(1))
    return pl.pallas_call(
        functools.partial(_out_ffn_kernel, alpha=alpha),
        grid=(n_tok // tm,),
        in_specs=[tok(D_MODEL), tok(oa.shape[1]), tok(ob.shape[1]), tok(oc.shape[1]),
                  of_layer(wo), once(g1), once(b1), once(g2), once(b2),
                  of_layer(wg), of_layer(wu), of_layer(wd)],
        out_specs=tok(D_MODEL),
        out_shape=jax.ShapeDtypeStruct((n_tok, D_MODEL), F32),
        compiler_params=pltpu.CompilerParams(dimension_semantics=("arbitrary",),
                                             vmem_limit_bytes=VMEM_LIMIT),
        name="out_ffn",
    )(x2d, oa, ob, oc, wo, g1, b1, g2, b2, wg, wu, wd)


def _rotate_half_cols(w):
    half = w.shape[-1] // 2
    return jnp.concatenate([-w[..., half:], w[..., :half]], axis=-1)


def _place_rope(w):
    return jnp.pad(w, ((0, 0), (C_NOPE, LANES - C_NOPE - C_ROPE)))


def _layer_weights(w_in, w_uq, w_ukv):
    main = A_COLS + B_COLS + C_Q_LORA + C_KV_LORA
    k_r = w_in[:, main:]
    w1 = jnp.concatenate([w_in[:, :main], _place_rope(k_r), _place_rope(_rotate_half_cols(k_r))],
                         axis=1).astype(BF16)
    rows = w_uq.shape[0]
    uq = w_uq.reshape(rows, C_HEADS, C_NOPE + C_ROPE)
    nope, rope = uq[..., :C_NOPE], uq[..., C_NOPE:]
    tail = jnp.zeros((rows, C_HEADS, LANES - C_NOPE - C_ROPE), w_uq.dtype)
    q_main = jnp.concatenate([nope, rope, tail], axis=-1).reshape(rows, C_HEADS * LANES)
    q_half = jnp.concatenate([jnp.zeros_like(nope), _rotate_half_cols(rope), tail],
                             axis=-1).reshape(rows, C_HEADS * LANES)
    wq = jnp.concatenate([q_main, q_half], axis=1).astype(BF16)
    rows = w_ukv.shape[0]
    ukv = w_ukv.reshape(rows, C_HEADS, C_NOPE + HEAD_DIM)
    k_nope, v = ukv[..., :C_NOPE], ukv[..., C_NOPE:]
    k_main = jnp.concatenate([k_nope, jnp.zeros((rows, C_HEADS, LANES - C_NOPE), w_ukv.dtype)],
                             axis=-1).reshape(rows, C_HEADS * LANES)
    wkv = jnp.concatenate([k_main, v.reshape(rows, C_HEADS * HEAD_DIM)], axis=1).astype(BF16)
    return w1, wq, wkv


def _rope_tables(seq):
    half = C_ROPE // 2
    inv = ROPE_THETA ** (-np.arange(half, dtype=np.float64) / half)
    ang = np.arange(seq, dtype=np.float64)[:, None] * inv[None, :]
    cos = np.zeros((seq, LANES), np.float64)
    sin = np.zeros((seq, LANES), np.float64)
    cos[:, :C_NOPE] = 1.0
    cos[:, C_NOPE:C_NOPE + C_ROPE] = np.concatenate([np.cos(ang), np.cos(ang)], axis=1)
    sin[:, C_NOPE:C_NOPE + C_ROPE] = np.concatenate([np.sin(ang), np.sin(ang)], axis=1)
    return jnp.asarray(cos, F32), jnp.asarray(sin, F32)


def kernel(x, rel_bias, w_in, q_norm_g, kv_norm_g, w_uq, w_ukv, diff_lambda, subln_g, w_o, ln1_g, ln1_b, ln2_g, ln2_b, w_gate, w_up, w_down):
    batch, seq, d_model = x.shape
    depth = w_in.shape[0]
    alpha = (2 * depth) ** 0.25
    bias_a = _bias_tiles(rel_bias, A_HEADS, 0, len(A_PAIRS), A_BLK, 2 * A_BLK, True)
    bias_b = _bias_tiles(rel_bias, B_HEADS, A_HEADS, seq // BK, BQ, BK, False)
    cos_t, sin_t = _rope_tables(seq)
    wo_all, wg_all, wu_all, wd_all = (_to_bf16(w) for w in (w_o, w_gate, w_up, w_down))
    row = lambda a: a.reshape(1, -1)
    x2d = x.reshape(batch * seq, d_model)
    for l in range(depth):
        w1, wq, wkv = _layer_weights(w_in[l], w_uq[l], w_ukv[l])
        pa, pb, qc, kc, vc = _projection(x2d, w1, row(q_norm_g[l]), row(kv_norm_g[l]), wq, wkv,
                                         cos_t, sin_t, seq)
        lam_init = 0.8 - 0.6 * math.exp(-0.3 * l)
        oa = _attention_a(pa.reshape(batch, seq, -1), bias_a)
        ob = _attention_b(pb.reshape(batch, seq, -1), bias_b, diff_lambda[l],
                          row(jnp.concatenate([subln_g[l], subln_g[l]])), lam_init)
        oc = _attention_c(qc.reshape(batch, seq, -1), kc.reshape(batch, seq, -1),
                          vc.reshape(batch, seq, -1))
        flat = lambda a: a.reshape(batch * seq, -1)
        x2d = _out_ffn(x2d, flat(oa), flat(ob), flat(oc), l, wo_all,
                       row(ln1_g[l]), row(ln1_b[l]), row(ln2_g[l]), row(ln2_b[l]),
                       wg_all, wu_all, wd_all, alpha)
    return x2d.reshape(batch, seq, d_model)
```
